```python
import jax
import jax.numpy as jnp
from jax import lax
import numpy as np

D_MODEL = 1024
BATCH = 8
SEQ = 4096
DEPTH = 1

ATT_HEADS = 8
ATT_KV_HEADS = 2
ATT_HEAD_DIM = 128
ATT_GROUP = ATT_HEADS // ATT_KV_HEADS
ROPE_THETA = 500000.0
ROPE_FRACTION = 4
IDX_HEADS = 4
IDX_DIM = 64
TOPK_MAX = 256
Q_BLOCK = 128
SSM_EXPAND = 2
SSM_D_INNER = SSM_EXPAND * D_MODEL
SSM_HEAD_DIM = 64
SSM_HEADS = SSM_D_INNER // SSM_HEAD_DIM
SSM_GROUPS = 4
SSM_STATE = 128
SSM_CONV = 4
SSM_CHUNK = 128
N_BRANCHES = 2
EPS = 1e-6

ATT_Q_DIM = ATT_HEADS * ATT_HEAD_DIM
ATT_KV_DIM = ATT_KV_HEADS * ATT_HEAD_DIM
ATT_ROT_DIM = ATT_HEAD_DIM // ROPE_FRACTION
IDX_ROT_DIM = IDX_DIM // ROPE_FRACTION
SSM_CONV_DIM = SSM_D_INNER + 2 * SSM_GROUPS * SSM_STATE
IN_SPLITS = (ATT_Q_DIM, ATT_KV_DIM, ATT_KV_DIM, ATT_Q_DIM, IDX_HEADS * IDX_DIM, IDX_DIM, IDX_HEADS,
             SSM_D_INNER, SSM_CONV_DIM, SSM_HEADS, N_BRANCHES * D_MODEL)
IN_DIM = sum(IN_SPLITS)

kernel_name = "hybrid_dsa_mamba2_gated_block"


def _split_points():
    pts, acc = [], 0
    for s in IN_SPLITS[:-1]:
        acc += s
        pts.append(acc)
    return pts


def rms_norm(x, g):
    xf = x.astype(jnp.float32)
    y = xf * lax.rsqrt(jnp.mean(xf * xf, axis=-1, keepdims=True) + EPS)
    return y.astype(x.dtype) * g


def layer_norm(x, g, b):
    xf = x.astype(jnp.float32)
    mu = jnp.mean(xf, axis=-1, keepdims=True)
    var = jnp.mean(jnp.square(xf - mu), axis=-1, keepdims=True)
    return ((xf - mu) * lax.rsqrt(var + EPS)).astype(x.dtype) * g + b


def rope_tables(positions, rot_dim):
    inv = jnp.power(ROPE_THETA, -(jnp.arange(0, rot_dim, 2, dtype=jnp.float32) / rot_dim))
    ang = positions.astype(jnp.float32)[..., None] * inv
    return jnp.cos(ang)[:, :, None, :], jnp.sin(ang)[:, :, None, :]


def partial_rope(x, cos, sin):
    half = cos.shape[-1]
    rot = 2 * half
    x1 = x[..., :half].astype(jnp.float32)
    x2 = x[..., half:rot].astype(jnp.float32)
    r1 = x1 * cos - x2 * sin
    r2 = x2 * cos + x1 * sin
    return jnp.concatenate([r1.astype(x.dtype), r2.astype(x.dtype), x[..., rot:]], axis=-1)


def causal_depthwise_conv(u, w, b):
    ch = u.shape[-1]
    y = lax.conv_general_dilated(u, w[:, None, :].astype(u.dtype), window_strides=(1,),
                                 padding=[(SSM_CONV - 1, 0)],
                                 dimension_numbers=("NWC", "WIO", "NWC"),
                                 feature_group_count=ch)
    return y + b


def segsum(a):
    n = a.shape[-1]
    ar = jnp.broadcast_to(a[..., :, None], a.shape + (n,))
    ar = jnp.where(jnp.tril(jnp.ones((n, n), dtype=bool), -1), ar, 0.0)
    cs = jnp.cumsum(ar, axis=-2)
    return jnp.where(jnp.tril(jnp.ones((n, n), dtype=bool)), cs, -jnp.inf)


def ssd(xh, dt, a, bm, cm):
    bsz, s, h, p = xh.shape
    g, n = bm.shape[2], bm.shape[3]
    r = h // g
    nc, ln = s // SSM_CHUNK, SSM_CHUNK
    xc = (xh * dt[..., None]).reshape(bsz, nc, ln, g, r, p)
    adt = (dt * a).reshape(bsz, nc, ln, g, r).transpose(0, 1, 3, 4, 2)
    a_cum = jnp.cumsum(adt, axis=-1)
    bc = bm.reshape(bsz, nc, ln, g, n)
    cc = cm.reshape(bsz, nc, ln, g, n)
    decay = jnp.exp(segsum(adt))
    cb = jnp.einsum("bclgn,bcsgn->bcgls", cc, bc)
    y_diag = jnp.einsum("bcgls,bcgrls,bcsgrp->bclgrp", cb, decay, xc)
    decay_states = jnp.exp(a_cum[..., -1:] - a_cum)
    states = jnp.einsum("bclgn,bcgrl,bclgrp->bcgrpn", bc, decay_states, xc)
    chunk_decay = jnp.exp(a_cum[..., -1])

    def step(hstate, inp):
        st, dec = inp
        return dec[..., None, None] * hstate + st, hstate

    h0 = jnp.zeros((bsz, g, r, p, n), dtype=xc.dtype)
    _, prev = lax.scan(step, h0, (states.swapaxes(0, 1), chunk_decay.swapaxes(0, 1)))
    prev = prev.swapaxes(0, 1)
    y_off = jnp.einsum("bclgn,bcgrpn,bcgrl->bclgrp", cc, prev, jnp.exp(a_cum))
    return (y_diag + y_off).reshape(bsz, s, h, p)


def sparse_attention(q, k, v, q_idx, k_idx, w_idx):
    bsz, s = q.shape[0], q.shape[1]
    topk = min(TOPK_MAX, s // 4)
    nblk = s // Q_BLOCK
    scale = ATT_HEAD_DIM ** -0.5
    key_pos = jnp.arange(s)
    gather = jax.vmap(lambda arr, idx: arr[idx])

    def to_blocks(arr):
        return arr.reshape((bsz, nblk, Q_BLOCK) + arr.shape[2:]).swapaxes(0, 1)

    qg = q.reshape(bsz, s, ATT_KV_HEADS, ATT_GROUP, ATT_HEAD_DIM)

    def block(args):
        blk, qb, qib, wb = args
        qpos = blk * Q_BLOCK + jnp.arange(Q_BLOCK)
        causal = key_pos[None, :] <= qpos[:, None]
        logits = jnp.einsum("bthd,bsd->bths", qib, k_idx)
        score = jnp.einsum("bth,bths->bts", wb, jax.nn.relu(logits))
        score = jnp.where(causal[None], score, -jnp.inf)
        _, sel = lax.top_k(score, topk)
        valid = sel <= qpos[None, :, None]
        ks = gather(k, sel)
        vs = gather(v, sel)
        sc = jnp.einsum("btgrd,btjgd->btgrj", qb, ks).astype(jnp.float32) * scale
        sc = jnp.where(valid[:, :, None, None, :], sc, -jnp.inf)
        pr = jax.nn.softmax(sc, axis=-1).astype(vs.dtype)
        return jnp.einsum("btgrj,btjgd->btgrd", pr, vs)

    out = lax.map(block, (jnp.arange(nblk), to_blocks(qg), to_blocks(q_idx), to_blocks(w_idx)))
    return out.swapaxes(0, 1).reshape(bsz, s, ATT_Q_DIM)


def setup_inputs(seed: int = 0) -> dict:
    key = jax.random.key(seed)
    ks = jax.random.split(key, 24)
    nrm = jax.random.normal
    f32 = jnp.float32
    dt0 = jnp.exp(jax.random.uniform(ks[14], (DEPTH, SSM_HEADS), dtype=f32)
                  * (np.log(0.1) - np.log(0.001)) + np.log(0.001))
    return {
        "x": nrm(ks[0], (BATCH, SEQ, D_MODEL), f32),
        "c": nrm(ks[1], (BATCH, D_MODEL), f32),
        "positions": jnp.broadcast_to(jnp.arange(SEQ, dtype=jnp.int32)[None, :], (BATCH, SEQ)),
        "ada_w": nrm(ks[2], (DEPTH, D_MODEL, 3 * D_MODEL), f32) * (0.5 * D_MODEL ** -0.5),
        "ada_b": nrm(ks[3], (DEPTH, 3 * D_MODEL), f32) * 0.01,
        "norm_g": 1.0 + 0.1 * nrm(ks[4], (DEPTH, D_MODEL), f32),
        "w_in": nrm(ks[5], (DEPTH, D_MODEL, IN_DIM), f32) * D_MODEL ** -0.5,
        "q_norm_g": 1.0 + 0.1 * nrm(ks[6], (DEPTH, ATT_HEAD_DIM), f32),
        "k_norm_g": 1.0 + 0.1 * nrm(ks[7], (DEPTH, ATT_HEAD_DIM), f32),
        "idx_k_ln_g": 1.0 + 0.1 * nrm(ks[8], (DEPTH, IDX_DIM), f32),
        "idx_k_ln_b": 0.01 * nrm(ks[9], (DEPTH, IDX_DIM), f32),
        "conv_w": nrm(ks[10], (DEPTH, SSM_CONV, SSM_CONV_DIM), f32) * SSM_CONV ** -0.5,
        "conv_b": 0.01 * nrm(ks[11], (DEPTH, SSM_CONV_DIM), f32),
        "dt_bias": dt0 + jnp.log(-jnp.expm1(-dt0)),
        "a_log": jnp.log(jax.random.uniform(ks[12], (DEPTH, SSM_HEADS), f32, 1.0, 16.0)),
        "d_skip": 1.0 + 0.1 * nrm(ks[13], (DEPTH, SSM_HEADS), f32),
        "ssm_norm_g": 1.0 + 0.1 * nrm(ks[15], (DEPTH, SSM_D_INNER), f32),
        "w_branch_att": nrm(ks[16], (DEPTH, ATT_Q_DIM, D_MODEL), f32) * ATT_Q_DIM ** -0.5,
        "w_branch_ssm": nrm(ks[17], (DEPTH, SSM_D_INNER, D_MODEL), f32) * SSM_D_INNER ** -0.5,
        "w_out": nrm(ks[18], (DEPTH, D_MODEL, D_MODEL), f32) * D_MODEL ** -0.5,
    }


def reference(x, c, positions, ada_w, ada_b, norm_g, w_in, q_norm_g, k_norm_g, idx_k_ln_g, idx_k_ln_b,
              conv_w, conv_b, dt_bias, a_log, d_skip, ssm_norm_g, w_branch_att, w_branch_ssm, w_out):
    f32 = jnp.float32
    bsz, s, _ = x.shape
    cos_a, sin_a = rope_tables(positions, ATT_ROT_DIM)
    cos_i, sin_i = rope_tables(positions, IDX_ROT_DIM)
    pts = _split_points()
    for l in range(DEPTH):
        mod = jax.nn.silu(c) @ ada_w[l] + ada_b[l]
        shift, scale, gate = jnp.split(mod, 3, axis=-1)
        h = rms_norm(x, norm_g[l]) * (1.0 + scale[:, None, :]) + shift[:, None, :]
        proj = h @ w_in[l]
        (q, k, v, z_att, q_idx, k_idx, w_idx, z_ssm, xbc, dt_raw, gate_logits) = jnp.split(proj, pts, axis=-1)

        q = partial_rope(rms_norm(q.reshape(bsz, s, ATT_HEADS, ATT_HEAD_DIM), q_norm_g[l]), cos_a, sin_a)
        k = partial_rope(rms_norm(k.reshape(bsz, s, ATT_KV_HEADS, ATT_HEAD_DIM), k_norm_g[l]), cos_a, sin_a)
        v = v.reshape(bsz, s, ATT_KV_HEADS, ATT_HEAD_DIM)
        q_idx = partial_rope(q_idx.reshape(bsz, s, IDX_HEADS, IDX_DIM), cos_i, sin_i) * (IDX_DIM ** -0.5)
        k_idx = partial_rope(layer_norm(k_idx, idx_k_ln_g[l], idx_k_ln_b[l])[:, :, None, :], cos_i, sin_i)[:, :, 0, :]
        w_idx = w_idx * (IDX_HEADS ** -0.5)
        o_att = sparse_attention(q, k, v, q_idx, k_idx, w_idx) * jax.nn.silu(z_att)
        y_att = o_att @ w_branch_att[l]

        xbc = jax.nn.silu(causal_depthwise_conv(xbc, conv_w[l], conv_b[l]))
        xs, bm, cm = jnp.split(xbc, [SSM_D_INNER, SSM_D_INNER + SSM_GROUPS * SSM_STATE], axis=-1)
        xs = xs.reshape(bsz, s, SSM_HEADS, SSM_HEAD_DIM).astype(f32)
        dt = jax.nn.softplus(dt_raw.astype(f32) + dt_bias[l].astype(f32))
        a = -jnp.exp(a_log[l].astype(f32))
        y = ssd(xs, dt, a,
                bm.reshape(bsz, s, SSM_GROUPS, SSM_STATE).astype(f32),
                cm.reshape(bsz, s, SSM_GROUPS, SSM_STATE).astype(f32))
        y = y + d_skip[l].astype(f32)[:, None] * xs
        y = y.reshape(bsz, s, SSM_D_INNER).astype(x.dtype) * jax.nn.silu(z_ssm)
        y = rms_norm(y.reshape(bsz, s, SSM_GROUPS, SSM_D_INNER // SSM_GROUPS),
                     ssm_norm_g[l].reshape(SSM_GROUPS, SSM_D_INNER // SSM_GROUPS)).reshape(bsz, s, SSM_D_INNER)
        y_ssm = y @ w_branch_ssm[l]

        g_att, g_ssm = jnp.split(gate_logits, 2, axis=-1)
        merged = jax.nn.sigmoid(g_att) * y_att + jax.nn.sigmoid(g_ssm) * y_ssm
        x = x + gate[:, None, :] * (merged @ w_out[l])
    return x
```

```python
import functools

import jax
import jax.numpy as jnp
from jax import lax
from jax.experimental import pallas as pl
from jax.experimental.pallas import tpu as pltpu

F32 = jnp.float32
MXU_DTYPE = jnp.bfloat16
ACT_DTYPE = jnp.bfloat16

D_MODEL = 1024
ATT_HEADS = 8
ATT_KV_HEADS = 2
ATT_HEAD_DIM = 128
ATT_GROUP = ATT_HEADS // ATT_KV_HEADS
ROPE_THETA = 500000.0
ATT_ROT_DIM = 32
IDX_HEADS = 4
IDX_DIM = 64
IDX_ROT_DIM = 16
TOPK_MAX = 256
SSM_D_INNER = 2048
SSM_HEAD_DIM = 64
SSM_HEADS = 32
SSM_GROUPS = 4
SSM_STATE = 128
SSM_CONV = 4
SSM_CHUNK = 128
SSM_CONV_DIM = SSM_D_INNER + 2 * SSM_GROUPS * SSM_STATE
EPS = 1e-6

ATT_Q_DIM = ATT_HEADS * ATT_HEAD_DIM
ATT_KV_DIM = ATT_KV_HEADS * ATT_HEAD_DIM
IDX_Q_DIM = IDX_HEADS * IDX_DIM
LANE = 128
SUBLANE = 8
INT_MIN = -(2 ** 31)
NEG_BIG = -1e30
VMEM_LIMIT = 56 * 1024 * 1024

COL_XBC = 0
COL_Q = COL_XBC + SSM_CONV_DIM
COL_ZSSM = COL_Q + ATT_Q_DIM
COL_GATE = COL_ZSSM + SSM_D_INNER
COL_ZATT = COL_GATE + 2 * D_MODEL
COL_K = COL_ZATT + ATT_Q_DIM
COL_V = COL_K + ATT_KV_DIM
COL_QIDX = COL_V + ATT_KV_DIM
COL_KIDX = COL_QIDX + IDX_Q_DIM
WIDE_COLS = 10240
SMALL_DT = 0
SMALL_WIDX = SSM_HEADS


def _split3(x):
    hi = x.astype(jnp.bfloat16)
    r1 = x - hi.astype(F32)
    mid = r1.astype(jnp.bfloat16)
    lo = (r1 - mid.astype(F32)).astype(jnp.bfloat16)
    return hi, mid, lo


def _dot(a, b):
    return jnp.dot(a, b, preferred_element_type=F32)


def _dot_f32(a, b):
    a0, a1, a2 = _split3(a)
    b0, b1, b2 = _split3(b)
    return (_dot(a0, b0) + (_dot(a0, b1) + _dot(a1, b0))
            + (_dot(a0, b2) + _dot(a1, b1) + _dot(a2, b0)))


def _dot_x_sel(x, sel):
    m = x.shape[0]
    parts = jnp.concatenate(_split3(x), axis=0)
    r = _dot(parts, sel)
    return r[:m] + r[m:2 * m] + r[2 * m:]


def _dot_sel_x(sel, x):
    hi, mid, lo = _split3(x)
    return _dot(sel, hi) + _dot(sel, mid) + _dot(sel, lo)


def _sigmoid(x):
    return 1.0 / (1.0 + jnp.exp(-x))


def _silu(x):
    return x * _sigmoid(x)


def _softplus(x):
    return jnp.maximum(x, 0.0) + jnp.log(1.0 + jnp.exp(-jnp.abs(x)))


def _ada_kernel(c_ref, w_ref, b_ref, o_ref):
    o_ref[...] = _dot_f32(_silu(c_ref[...]), w_ref[...]) + b_ref[...]


def _ada(c, ada_w, ada_b):
    bsz = c.shape[0]
    return pl.pallas_call(
        _ada_kernel,
        out_shape=jax.ShapeDtypeStruct((bsz, 3 * D_MODEL), F32),
        grid=(3,),
        in_specs=[pl.BlockSpec((bsz, D_MODEL), lambda j: (0, 0)),
                  pl.BlockSpec((D_MODEL, D_MODEL), lambda j: (0, j)),
                  pl.BlockSpec((1, D_MODEL), lambda j: (0, j))],
        out_specs=pl.BlockSpec((bsz, D_MODEL), lambda j: (0, j)),
        compiler_params=pltpu.CompilerParams(dimension_semantics=("arbitrary",),
                                             vmem_limit_bytes=VMEM_LIMIT),
        name="ada",
    )(c, ada_w, ada_b.reshape(1, -1))


def _norm_kernel(x_ref, g_ref, scale_ref, shift_ref, wsm_ref, h_ref, small_ref):
    x = x_ref[...]
    ms = jnp.mean(x * x, axis=-1, keepdims=True)
    y = x * lax.rsqrt(ms + EPS) * g_ref[...]
    h = y * (1.0 + scale_ref[...]) + shift_ref[...]
    h_ref[...] = h.astype(h_ref.dtype)
    small_ref[...] = _dot_f32(h, wsm_ref[...])


def _norm(x, norm_g, scale, shift, w_small, tm):
    bsz, s, d = x.shape
    return pl.pallas_call(
        _norm_kernel,
        out_shape=(jax.ShapeDtypeStruct((bsz, s, d), ACT_DTYPE),
                   jax.ShapeDtypeStruct((bsz, s, LANE), F32)),
        grid=(bsz, s // tm),
        in_specs=[pl.BlockSpec((None, tm, d), lambda b, i: (b, i, 0)),
                  pl.BlockSpec((1, d), lambda b, i: (0, 0)),
                  pl.BlockSpec((None, 1, d), lambda b, i: (b, 0, 0)),
                  pl.BlockSpec((None, 1, d), lambda b, i: (b, 0, 0)),
                  pl.BlockSpec((d, LANE), lambda b, i: (0, 0))],
        out_specs=(pl.BlockSpec((None, tm, d), lambda b, i: (b, i, 0)),
                   pl.BlockSpec((None, tm, LANE), lambda b, i: (b, i, 0))),
        compiler_params=pltpu.CompilerParams(dimension_semantics=("parallel", "parallel"),
                                             vmem_limit_bytes=VMEM_LIMIT),
        name="norm",
    )(x, norm_g.reshape(1, d), scale[:, None, :], shift[:, None, :], w_small)


def _proj_kernel(h_ref, w_ref, o_ref):
    o_ref[...] = _dot(h_ref[...], w_ref[...]).astype(o_ref.dtype)


def _proj(h2d, w_wide, tm, tn):
    t, d = h2d.shape
    n = w_wide.shape[1]
    return pl.pallas_call(
        _proj_kernel,
        out_shape=jax.ShapeDtypeStruct((t, n), ACT_DTYPE),
        grid=(t // tm, n // tn),
        in_specs=[pl.BlockSpec((tm, d), lambda i, j: (i, 0)),
                  pl.BlockSpec((d, tn), lambda i, j: (0, j))],
        out_specs=pl.BlockSpec((tm, tn), lambda i, j: (i, j)),
        compiler_params=pltpu.CompilerParams(dimension_semantics=("parallel", "arbitrary"),
                                             vmem_limit_bytes=VMEM_LIMIT),
        name="proj",
    )(h2d, w_wide)


def _rope(y, cos_t, sin_t, first_half, half):
    fwd = pltpu.roll(y, LANE - half, 1)
    bwd = pltpu.roll(y, half, 1)
    return y * cos_t + jnp.where(first_half, fwd, bwd) * sin_t


def _prep_kernel(q_ref, k_ref, qi_ref, ki_ref, ca_ref, sa_ref, ci_ref, si_ref,
                 qg_ref, kg_ref, lg_ref, lb_ref, qo_ref, ko_ref, qio_ref, kio_ref):
    tm = q_ref.shape[0]
    lane = lax.broadcasted_iota(jnp.int32, (tm, LANE), 1)
    ca, sa = ca_ref[...], sa_ref[...]
    ci, si = ci_ref[...], si_ref[...]
    att_first = lane < ATT_ROT_DIM // 2
    idx_first = (lane % IDX_DIM) < IDX_ROT_DIM // 2

    def norm_rope(x, g):
        ms = jnp.mean(x * x, axis=-1, keepdims=True)
        y = x * lax.rsqrt(ms + EPS) * g
        return _rope(y, ca, sa, att_first, ATT_ROT_DIM // 2)

    q_scale = ATT_HEAD_DIM ** -0.5
    for h in range(ATT_HEADS):
        sl = slice(h * LANE, (h + 1) * LANE)
        qo_ref[:, sl] = (norm_rope(q_ref[:, sl].astype(F32), qg_ref[...]) * q_scale).astype(qo_ref.dtype)
    for h in range(ATT_KV_HEADS):
        sl = slice(h * LANE, (h + 1) * LANE)
        ko_ref[:, sl] = norm_rope(k_ref[:, sl].astype(F32), kg_ref[...]).astype(ko_ref.dtype)
    for j in range(IDX_Q_DIM // LANE):
        sl = slice(j * LANE, (j + 1) * LANE)
        y = _rope(qi_ref[:, sl].astype(F32), ci, si, idx_first, IDX_ROT_DIM // 2)
        qio_ref[:, sl] = (y * IDX_DIM ** -0.5).astype(qio_ref.dtype)
    x = ki_ref[...].astype(F32)
    real = lane < IDX_DIM
    mu = jnp.sum(jnp.where(real, x, 0.0), axis=-1, keepdims=True) * (1.0 / IDX_DIM)
    dlt = jnp.where(real, x - mu, 0.0)
    var = jnp.sum(dlt * dlt, axis=-1, keepdims=True) * (1.0 / IDX_DIM)
    y = dlt * lax.rsqrt(var + EPS) * lg_ref[...] + lb_ref[...]
    kio_ref[...] = _rope(y, ci, si, idx_first, IDX_ROT_DIM // 2).astype(kio_ref.dtype)


def _prep(proj, tabs, q_norm_g, k_norm_g, ln_g, ln_b, tm):
    t = proj.shape[0]
    ca, sa, ci, si = tabs
    row = lambda w, c0: pl.BlockSpec((tm, w), lambda i: (i, c0 // w))
    tab = pl.BlockSpec((tm, LANE), lambda i: (i, 0))
    par = pl.BlockSpec((1, LANE), lambda i: (0, 0))
    pad = lambda v: jnp.pad(v, (0, LANE - v.shape[0])).reshape(1, LANE)
    out = lambda w: pl.BlockSpec((tm, w), lambda i: (i, 0))
    return pl.pallas_call(
        _prep_kernel,
        out_shape=(jax.ShapeDtypeStruct((t, ATT_Q_DIM), MXU_DTYPE),
                   jax.ShapeDtypeStruct((t, ATT_KV_DIM), MXU_DTYPE),
                   jax.ShapeDtypeStruct((t, IDX_Q_DIM), MXU_DTYPE),
                   jax.ShapeDtypeStruct((t, LANE), MXU_DTYPE)),
        grid=(t // tm,),
        in_specs=[row(ATT_Q_DIM, COL_Q), row(ATT_KV_DIM, COL_K), row(IDX_Q_DIM, COL_QIDX), row(LANE, COL_KIDX),
                  tab, tab, tab, tab, par, par, par, par],
        out_specs=(out(ATT_Q_DIM), out(ATT_KV_DIM), out(IDX_Q_DIM), out(LANE)),
        compiler_params=pltpu.CompilerParams(dimension_semantics=("parallel",),
                                             vmem_limit_bytes=VMEM_LIMIT),
        name="prep",
    )(proj, proj, proj, proj, ca, sa, ci, si,
      q_norm_g.reshape(1, LANE), k_norm_g.reshape(1, LANE), pad(ln_g), pad(ln_b))


def _attn_kernel(qT_ref, qiT_ref, smT_ref, k_ref, vT_ref, ki_ref, o_ref,
                 keys_ref, m_ref, l_ref, acc_ref, *, tq, topk):
    i = pl.program_id(1)
    nchunk = i + 1
    tk = tq
    w_idx = smT_ref[SMALL_WIDX:SMALL_WIDX + IDX_HEADS, :] * (IDX_HEADS ** -0.5)
    qi_all = jnp.concatenate([qiT_ref[h * IDX_DIM:(h + 1) * IDX_DIM, :] for h in range(IDX_HEADS)], axis=1)
    row_iota = lax.broadcasted_iota(jnp.int32, (tk, tq), 0)
    col_iota = lax.broadcasted_iota(jnp.int32, (tk, tq), 1)

    def score_body(c, carry):
        r0 = pl.multiple_of(c * tk, tk)
        kc = ki_ref[pl.ds(r0, tk), :][:, :IDX_DIM]
        lg = _dot(kc, qi_all)
        s = w_idx[0:1, :] * jnp.maximum(lg[:, :tq], 0.0)
        for h in range(1, IDX_HEADS):
            s = s + w_idx[h:h + 1, :] * jnp.maximum(lg[:, h * tq:(h + 1) * tq], 0.0)
        bits = lax.bitcast_convert_type(s + 0.0, jnp.int32)
        key = jnp.where(bits < 0, bits ^ jnp.int32(0x7FFFFFFF), bits)
        causal = (r0 + row_iota) <= (i * tq + col_iota)
        keys_ref[pl.ds(r0, tk), :] = jnp.where(causal, key, jnp.int32(INT_MIN))
        return carry

    lax.fori_loop(0, nchunk, score_body, 0)

    def count(pred):
        def body(c, acc):
            kk = keys_ref[pl.ds(pl.multiple_of(c * tk, tk), tk), :]
            hit = jnp.where(pred(kk), jnp.int32(1), jnp.int32(0))
            return acc + hit.reshape(tk // SUBLANE, SUBLANE, tq).sum(axis=0)
        acc = lax.fori_loop(0, nchunk, body, jnp.zeros((SUBLANE, tq), jnp.int32))
        return acc.sum(axis=0, keepdims=True)

    def bit_body(j, ans):
        trial = ans + lax.shift_left(jnp.int32(1), 31 - j)
        cnt = count(lambda kk: kk >= trial)
        return jnp.where(cnt >= topk, trial, ans)

    ans = lax.fori_loop(0, 32, bit_body, jnp.full((1, tq), INT_MIN, jnp.int32))
    thr = jnp.maximum(ans, jnp.int32(INT_MIN + 1))
    cnt_gt = count(lambda kk: kk > thr)
    cnt_eq = count(lambda kk: kk == thr)
    need = topk - cnt_gt
    any_surplus_tie = jnp.max(jnp.where(cnt_eq > need, 1, 0)) > 0

    m_ref[...] = jnp.full(m_ref.shape, NEG_BIG, F32)
    l_ref[...] = jnp.zeros(l_ref.shape, F32)
    acc_ref[...] = jnp.zeros(acc_ref.shape, F32)

    def attend(c, bias):
        r0 = pl.multiple_of(c * tk, tk)
        for h in range(ATT_HEADS):
            g = h // ATT_GROUP
            kc = k_ref[pl.ds(r0, tk), g * LANE:(g + 1) * LANE]
            s = _dot(kc, qT_ref[h * LANE:(h + 1) * LANE, :]) + bias
            m_old = m_ref[h]
            m_new = jnp.maximum(m_old, jnp.max(s, axis=0, keepdims=True))
            alpha = jnp.exp(m_old - m_new)
            p = jnp.exp(s - m_new)
            l_ref[h] = alpha * l_ref[h] + jnp.sum(p, axis=0, keepdims=True)
            vc = vT_ref[g * LANE:(g + 1) * LANE, pl.ds(r0, tk)]
            acc_ref[h] = alpha * acc_ref[h] + _dot(vc, p.astype(MXU_DTYPE))
            m_ref[h] = m_new

    def fast_body(c, carry):
        kk = keys_ref[pl.ds(pl.multiple_of(c * tk, tk), tk), :]
        attend(c, jnp.where(kk >= thr, 0.0, NEG_BIG))
        return carry

    def tie_body(c, taken):
        kk = keys_ref[pl.ds(pl.multiple_of(c * tk, tk), tk), :]
        eq = kk == thr
        eq_b = jnp.where(eq, 1.0, 0.0).astype(jnp.bfloat16)
        before = jnp.where(col_iota < row_iota, 1.0, 0.0).astype(jnp.bfloat16)
        rank = _dot(before, eq_b) + taken
        sel = (kk > thr) | (eq & (rank < need.astype(F32)))
        attend(c, jnp.where(sel, 0.0, NEG_BIG))
        return taken + jnp.sum(eq_b.astype(F32), axis=0, keepdims=True)

    @pl.when(jnp.logical_not(any_surplus_tie))
    def _():
        lax.fori_loop(0, nchunk, fast_body, 0)

    @pl.when(any_surplus_tie)
    def _():
        lax.fori_loop(0, nchunk, tie_body, jnp.zeros((1, tq), F32))

    for h in range(ATT_HEADS):
        o_ref[h * LANE:(h + 1) * LANE, :] = (acc_ref[h] / l_ref[h]).astype(o_ref.dtype)


def _attn(qT, qiT, smallT, k, vT, ki, tq):
    bsz, _, s = qT.shape
    topk = min(TOPK_MAX, s // 4)
    kern = functools.partial(_attn_kernel, tq=tq, topk=topk)
    qtile = lambda r: pl.BlockSpec((None, r, tq), lambda b, i: (b, 0, i))
    return pl.pallas_call(
        kern,
        out_shape=jax.ShapeDtypeStruct((bsz, ATT_Q_DIM, s), ACT_DTYPE),
        grid=(bsz, s // tq),
        in_specs=[qtile(ATT_Q_DIM), qtile(IDX_Q_DIM), qtile(LANE),
                  pl.BlockSpec((None, s, ATT_KV_DIM), lambda b, i: (b, 0, 0)),
                  pl.BlockSpec((None, ATT_KV_DIM, s), lambda b, i: (b, 0, 0)),
                  pl.BlockSpec((None, s, LANE), lambda b, i: (b, 0, 0))],
        out_specs=qtile(ATT_Q_DIM),
        scratch_shapes=[pltpu.VMEM((s, tq), jnp.int32),
                        pltpu.VMEM((ATT_HEADS, 1, tq), F32),
                        pltpu.VMEM((ATT_HEADS, 1, tq), F32),
                        pltpu.VMEM((ATT_HEADS, ATT_HEAD_DIM, tq), F32)],
        compiler_params=pltpu.CompilerParams(dimension_semantics=("parallel", "arbitrary"),
                                             vmem_limit_bytes=VMEM_LIMIT),
        name="attn",
    )(qT, qiT, smallT, k, vT, ki)


def _ssm_kernel(xbc_ref, z_ref, sm_ref, smT_ref, cw_ref, cb_ref, dtb_ref, dtbT_ref, a_ref, aT_ref,
                dsk_ref, ng_ref, exp_ref, y_ref, ext_ref, st_ref, yacc_ref):
    ln = SSM_CHUNK
    c = pl.program_id(1)

    @pl.when(c == 0)
    def _():
        ext_ref[0:SUBLANE, :] = jnp.zeros((SUBLANE, SSM_CONV_DIM), F32)
        st_ref[...] = jnp.zeros(st_ref.shape, F32)

    raw = xbc_ref[...].astype(F32)
    ext_ref[SUBLANE:SUBLANE + ln, :] = raw
    acc = cb_ref[...] + cw_ref[0:1, :] * ext_ref[SUBLANE - 3:SUBLANE - 3 + ln, :]
    for t in range(1, SSM_CONV):
        acc = acc + cw_ref[t:t + 1, :] * ext_ref[SUBLANE - 3 + t:SUBLANE - 3 + t + ln, :]
    ext_ref[0:SUBLANE, :] = raw[ln - SUBLANE:, :]
    xbc = _silu(acc)
    xs = xbc[:, :SSM_D_INNER]
    b_all = xbc[:, SSM_D_INNER:SSM_D_INNER + SSM_GROUPS * SSM_STATE].astype(MXU_DTYPE)
    c_all = xbc[:, SSM_D_INNER + SSM_GROUPS * SSM_STATE:].astype(MXU_DTYPE)

    dt = _softplus(sm_ref[...] + dtb_ref[...])
    adt = dt * a_ref[...]
    r_i = lax.broadcasted_iota(jnp.int32, (ln, ln), 0)
    c_i = lax.broadcasted_iota(jnp.int32, (ln, ln), 1)
    tril = r_i >= c_i
    incl = jnp.where(tril, 1.0, 0.0).astype(jnp.bfloat16)
    a_col = _dot_sel_x(incl, adt)
    dtT = _softplus(smT_ref[...] + dtbT_ref[...])
    incl_t = jnp.where(r_i <= c_i, 1.0, 0.0).astype(jnp.bfloat16)
    a_row = _dot_x_sel(dtT * aT_ref[...], incl_t)

    expand = exp_ref[...]
    wide = _dot_x_sel(jnp.concatenate([a_col, dt], axis=0), expand)
    a_wide, dt_wide = wide[:ln], wide[ln:]
    a_last = a_wide[ln - 1:ln, :]
    xc = xs * dt_wide
    xc_b = xc.astype(MXU_DTYPE)
    xd_b = (xc * jnp.exp(a_last - a_wide)).astype(MXU_DTYPE)
    e_wide = jnp.exp(a_wide)
    chunk_decay = e_wide[ln - 1:ln, :]

    lane = lax.broadcasted_iota(jnp.int32, (ln, LANE), 1)
    low = lane < SSM_HEAD_DIM
    gw = SSM_D_INNER // SSM_GROUPS
    for g in range(SSM_GROUPS):
        gs = slice(g * gw, (g + 1) * gw)
        b_g = b_all[:, g * SSM_STATE:(g + 1) * SSM_STATE]
        c_g = c_all[:, g * SSM_STATE:(g + 1) * SSM_STATE]
        cb = lax.dot_general(c_g, b_g, (((1,), (1,)), ((), ())), preferred_element_type=F32)
        y_off = _dot(c_g, st_ref[:, gs].astype(MXU_DTYPE)) * e_wide[:, gs]
        st_ref[:, gs] = chunk_decay[:, gs] * st_ref[:, gs] + lax.dot_general(
            b_g, xd_b[:, gs], (((0,), (0,)), ((), ())), preferred_element_type=F32)
        yacc_ref[:, gs] = y_off
        for jp in range(gw // LANE):
            j = g * (gw // LANE) + jp
            xp = xc_b[:, j * LANE:(j + 1) * LANE]
            yd = None
            for half, keep in ((0, low), (1, jnp.logical_not(low))):
                hd = 2 * j + half
                diff = a_col[:, hd:hd + 1] - a_row[hd:hd + 1, :]
                lmat = jnp.exp(jnp.where(tril, diff, -jnp.inf))
                mh = (cb * lmat).astype(MXU_DTYPE)
                part = _dot(mh, jnp.where(keep, xp, jnp.zeros_like(xp)))
                yd = part if yd is None else yd + part
            yacc_ref[:, j * LANE:(j + 1) * LANE] += yd

    y = yacc_ref[...] + dsk_ref[...] * xs
    y = y * _silu(z_ref[...].astype(F32))
    for g in range(SSM_GROUPS):
        gs = slice(g * gw, (g + 1) * gw)
        yg = y[:, gs]
        ms = jnp.mean(yg * yg, axis=-1, keepdims=True)
        y_ref[:, gs] = (yg * lax.rsqrt(ms + EPS) * ng_ref[:, gs]).astype(y_ref.dtype)


def _ssm(proj3, small, smallT, conv_w, conv_b, dt_bias, a_log, d_skip, ssm_norm_g):
    bsz, s, _ = proj3.shape
    ln = SSM_CHUNK
    padl = lambda v: jnp.pad(v, (0, LANE - v.shape[0]))
    a = padl(-jnp.exp(a_log))
    dtb = padl(dt_bias)
    head_of = jnp.arange(SSM_D_INNER) // SSM_HEAD_DIM
    expand = (jnp.arange(LANE)[:, None] == head_of[None, :]).astype(jnp.bfloat16)
    full = lambda shp: pl.BlockSpec(shp, lambda b, c: (0,) * len(shp))
    return pl.pallas_call(
        _ssm_kernel,
        out_shape=jax.ShapeDtypeStruct((bsz, s, SSM_D_INNER), ACT_DTYPE),
        grid=(bsz, s // ln),
        in_specs=[pl.BlockSpec((None, ln, SSM_CONV_DIM), lambda b, c: (b, c, COL_XBC // SSM_CONV_DIM)),
                  pl.BlockSpec((None, ln, SSM_D_INNER), lambda b, c: (b, c, COL_ZSSM // SSM_D_INNER)),
                  pl.BlockSpec((None, ln, LANE), lambda b, c: (b, c, 0)),
                  pl.BlockSpec((None, LANE, ln), lambda b, c: (b, 0, c)),
                  full((SSM_CONV, SSM_CONV_DIM)), full((1, SSM_CONV_DIM)),
                  full((1, LANE)), full((LANE, 1)), full((1, LANE)), full((LANE, 1)),
                  full((1, SSM_D_INNER)), full((1, SSM_D_INNER)), full((LANE, SSM_D_INNER))],
        out_specs=pl.BlockSpec((None, ln, SSM_D_INNER), lambda b, c: (b, c, 0)),
        scratch_shapes=[pltpu.VMEM((SUBLANE + ln, SSM_CONV_DIM), F32),
                        pltpu.VMEM((SSM_STATE, SSM_D_INNER), F32),
                        pltpu.VMEM((ln, SSM_D_INNER), F32)],
        compiler_params=pltpu.CompilerParams(dimension_semantics=("parallel", "arbitrary"),
                                             vmem_limit_bytes=VMEM_LIMIT),
        name="ssm",
    )(proj3, proj3, small, smallT, conv_w, conv_b.reshape(1, -1),
      dtb.reshape(1, LANE), dtb.reshape(LANE, 1), a.reshape(1, LANE), a.reshape(LANE, 1),
      jnp.repeat(d_skip, SSM_HEAD_DIM).reshape(1, -1), ssm_norm_g.reshape(1, -1), expand)


def _final_kernel(o_ref, za_ref, y_ref, ga_ref, gs_ref, x_ref, gate_ref, wa_ref, ws_ref, wo_ref, out_ref):
    o = (o_ref[...].astype(F32) * _silu(za_ref[...].astype(F32))).astype(MXU_DTYPE)
    y_att = _dot(o, wa_ref[...])
    y_ssm = _dot(y_ref[...], ws_ref[...])
    merged = _sigmoid(ga_ref[...].astype(F32)) * y_att + _sigmoid(gs_ref[...].astype(F32)) * y_ssm
    out_ref[...] = x_ref[...] + gate_ref[...] * _dot(merged.astype(MXU_DTYPE), wo_ref[...])


def _final(o3, proj3, y3, x, gate, w_att, w_ssm, w_out, tm):
    bsz, s, d = x.shape
    tile = lambda w, c0: pl.BlockSpec((None, tm, w), lambda b, i: (b, i, c0 // w))
    full = lambda shp: pl.BlockSpec(shp, lambda b, i: (0,) * len(shp))
    return pl.pallas_call(
        _final_kernel,
        out_shape=jax.ShapeDtypeStruct((bsz, s, d), x.dtype),
        grid=(bsz, s // tm),
        in_specs=[tile(ATT_Q_DIM, 0), tile(ATT_Q_DIM, COL_ZATT), tile(SSM_D_INNER, 0),
                  tile(D_MODEL, COL_GATE), tile(D_MODEL, COL_GATE + D_MODEL), tile(d, 0),
                  pl.BlockSpec((None, 1, d), lambda b, i: (b, 0, 0)),
                  full((ATT_Q_DIM, d)), full((SSM_D_INNER, d)), full((d, d))],
        out_specs=tile(d, 0),
        compiler_params=pltpu.CompilerParams(dimension_semantics=("parallel", "parallel"),
                                             vmem_limit_bytes=VMEM_LIMIT),
        name="final",
    )(o3, proj3, y3, proj3, proj3, x, gate[:, None, :],
      w_att.astype(MXU_DTYPE), w_ssm.astype(MXU_DTYPE), w_out.astype(MXU_DTYPE))


def _permute_w_in(w_in):
    sizes = (ATT_Q_DIM, ATT_KV_DIM, ATT_KV_DIM, ATT_Q_DIM, IDX_Q_DIM, IDX_DIM, IDX_HEADS,
             SSM_D_INNER, SSM_CONV_DIM, SSM_HEADS, 2 * D_MODEL)
    offs = [0]
    for sz in sizes:
        offs.append(offs[-1] + sz)
    q, k, v, z_att, q_idx, k_idx, w_idx, z_ssm, xbc, dt, gate = (
        w_in[:, offs[n]:offs[n + 1]] for n in range(len(sizes)))
    d = w_in.shape[0]
    zeros = lambda n: jnp.zeros((d, n), w_in.dtype)
    wide = jnp.concatenate(
        [xbc, q, z_ssm, gate, z_att, k, v, q_idx, k_idx, zeros(LANE - IDX_DIM),
         zeros(WIDE_COLS - COL_KIDX - LANE)], axis=1)
    small = jnp.concatenate([dt, w_idx, zeros(LANE - SSM_HEADS - IDX_HEADS)], axis=1)
    return wide.astype(MXU_DTYPE), small


def _rope_tables(positions):
    def angles(rot):
        inv = jnp.power(ROPE_THETA, -(jnp.arange(0, rot, 2, dtype=F32) / rot))
        ang = positions.astype(F32)[..., None] * inv
        return jnp.cos(ang), jnp.sin(ang)

    def lanes(cos, sin, width):
        half = cos.shape[-1]
        lead = cos.shape[:-1]
        ones = jnp.ones(lead + (width - 2 * half,), F32)
        ct = jnp.concatenate([cos, cos, ones], axis=-1)
        st = jnp.concatenate([-sin, sin, 0.0 * ones], axis=-1)
        reps = LANE // width
        return (jnp.tile(ct, reps).reshape(-1, LANE), jnp.tile(st, reps).reshape(-1, LANE))

    ca, sa = lanes(*angles(ATT_ROT_DIM), ATT_HEAD_DIM)
    ci, si = lanes(*angles(IDX_ROT_DIM), IDX_DIM)
    return ca, sa, ci, si


def _tile(n, pref):
    return pref if n % pref == 0 else n


def kernel(x, c, positions, ada_w, ada_b, norm_g, w_in, q_norm_g, k_norm_g, idx_k_ln_g, idx_k_ln_b,
           conv_w, conv_b, dt_bias, a_log, d_skip, ssm_norm_g, w_branch_att, w_branch_ssm, w_out):
    bsz, s, d = x.shape
    t = bsz * s
    tabs = _rope_tables(positions)
    for l in range(ada_w.shape[0]):
        mod = _ada(c, ada_w[l], ada_b[l])
        shift, scale, gate = mod[:, :d], mod[:, d:2 * d], mod[:, 2 * d:]
        w_wide, w_small = _permute_w_in(w_in[l])
        h, small = _norm(x, norm_g[l], scale, shift, w_small, _tile(s, 512))
        proj = _proj(h.reshape(t, d), w_wide, _tile(t, 1024), 1024)
        proj3 = proj.reshape(bsz, s, WIDE_COLS)

        q, k, q_idx, k_idx = _prep(proj, tabs, q_norm_g[l], k_norm_g[l], idx_k_ln_g[l], idx_k_ln_b[l],
                                   _tile(t, 512))
        to_t = lambda a: jnp.swapaxes(a.reshape(bsz, s, -1), 1, 2)
        smallT = jnp.swapaxes(small, 1, 2)
        oT = _attn(to_t(q), to_t(q_idx), smallT, k.reshape(bsz, s, -1),
                   jnp.swapaxes(proj3[:, :, COL_V:COL_V + ATT_KV_DIM], 1, 2),
                   k_idx.reshape(bsz, s, LANE), _tile(s, 256))
        o3 = jnp.swapaxes(oT, 1, 2)

        y3 = _ssm(proj3, small, smallT, conv_w[l], conv_b[l], dt_bias[l], a_log[l], d_skip[l], ssm_norm_g[l])
        x = _final(o3, proj3, y3, x, gate, w_branch_att[l], w_branch_ssm[l], w_out[l], _tile(s, 512))
    return x
```

```python
import functools

import jax
import jax.numpy as jnp
from jax import lax
from jax.experimental import pallas as pl
from jax.experimental.pallas import tpu as pltpu

F32 = jnp.float32
MXU_DTYPE = jnp.bfloat16
ACT_DTYPE = jnp.bfloat16

D_MODEL = 1024
ATT_HEADS = 8
ATT_KV_HEADS = 2
ATT_HEAD_DIM = 128
ATT_GROUP = ATT_HEADS // ATT_KV_HEADS
ROPE_THETA = 500000.0
ATT_ROT_DIM = 32
IDX_HEADS = 4
IDX_DIM = 64
IDX_ROT_DIM = 16
TOPK_MAX = 256
SSM_D_INNER = 2048
SSM_HEAD_DIM = 64
SSM_HEADS = 32
SSM_GROUPS = 4
SSM_STATE = 128
SSM_CONV = 4
SSM_CHUNK = 128
SSM_CONV_DIM = SSM_D_INNER + 2 * SSM_GROUPS * SSM_STATE
EPS = 1e-6

ATT_Q_DIM = ATT_HEADS * ATT_HEAD_DIM
ATT_KV_DIM = ATT_KV_HEADS * ATT_HEAD_DIM
IDX_Q_DIM = IDX_HEADS * IDX_DIM
LANE = 128
SUBLANE = 8
INT_MIN = -(2 ** 31)
NEG_BIG = -1e30
LOG2_E = 1.4426950408889634
SAFE_LOG2_LOGIT = 60.0
VMEM_LIMIT = 56 * 1024 * 1024

COL_XBC = 0
COL_Q = COL_XBC + SSM_CONV_DIM
COL_ZSSM = COL_Q + ATT_Q_DIM
COL_GATE = COL_ZSSM + SSM_D_INNER
COL_ZATT = COL_GATE + 2 * D_MODEL
COL_K = COL_ZATT + ATT_Q_DIM
COL_V = COL_K + ATT_KV_DIM
COL_QIDX = COL_V + ATT_KV_DIM
COL_KIDX = COL_QIDX + IDX_Q_DIM
WIDE_COLS = 10240
SMALL_DT = 0
SMALL_WIDX = SSM_HEADS


def _split3(x):
    hi = x.astype(jnp.bfloat16)
    r1 = x - hi.astype(F32)
    mid = r1.astype(jnp.bfloat16)
    lo = (r1 - mid.astype(F32)).astype(jnp.bfloat16)
    return hi, mid, lo


def _dot(a, b):
    return jnp.dot(a, b, preferred_element_type=F32)


def _dot_f32(a, b):
    a0, a1, a2 = _split3(a)
    b0, b1, b2 = _split3(b)
    return (_dot(a0, b0) + (_dot(a0, b1) + _dot(a1, b0))
            + (_dot(a0, b2) + _dot(a1, b1) + _dot(a2, b0)))


def _dot_x_sel(x, sel):
    m = x.shape[0]
    parts = jnp.concatenate(_split3(x), axis=0)
    r = _dot(parts, sel)
    return r[:m] + r[m:2 * m] + r[2 * m:]


def _dot_sel_x(sel, x):
    hi, mid, lo = _split3(x)
    return _dot(sel, hi) + _dot(sel, mid) + _dot(sel, lo)


def _sigmoid(x):
    return 1.0 / (1.0 + jnp.exp(-x))


def _silu(x):
    return x * _sigmoid(x)


def _softplus(x):
    return jnp.maximum(x, 0.0) + jnp.log(1.0 + jnp.exp(-jnp.abs(x)))


def _ada_kernel(c_ref, w_ref, b_ref, o_ref):
    o_ref[...] = _dot_f32(_silu(c_ref[...]), w_ref[...]) + b_ref[...]


def _ada(c, ada_w, ada_b):
    bsz = c.shape[0]
    return pl.pallas_call(
        _ada_kernel,
        out_shape=jax.ShapeDtypeStruct((bsz, 3 * D_MODEL), F32),
        grid=(3,),
        in_specs=[pl.BlockSpec((bsz, D_MODEL), lambda j: (0, 0)),
                  pl.BlockSpec((D_MODEL, D_MODEL), lambda j: (0, j)),
                  pl.BlockSpec((1, D_MODEL), lambda j: (0, j))],
        out_specs=pl.BlockSpec((bsz, D_MODEL), lambda j: (0, j)),
        compiler_params=pltpu.CompilerParams(dimension_semantics=("arbitrary",),
                                             vmem_limit_bytes=VMEM_LIMIT),
        name="ada",
    )(c, ada_w, ada_b.reshape(1, -1))


def _norm_kernel(x_ref, g_ref, scale_ref, shift_ref, wsm_ref, h_ref, small_ref):
    x = x_ref[...]
    ms = jnp.mean(x * x, axis=-1, keepdims=True)
    y = x * lax.rsqrt(ms + EPS) * g_ref[...]
    h = y * (1.0 + scale_ref[...]) + shift_ref[...]
    h_ref[...] = h.astype(h_ref.dtype)
    small_ref[...] = _dot_f32(h, wsm_ref[...])


def _norm(x, norm_g, scale, shift, w_small, tm):
    bsz, s, d = x.shape
    return pl.pallas_call(
        _norm_kernel,
        out_shape=(jax.ShapeDtypeStruct((bsz, s, d), ACT_DTYPE),
                   jax.ShapeDtypeStruct((bsz, s, LANE), F32)),
        grid=(bsz, s // tm),
        in_specs=[pl.BlockSpec((None, tm, d), lambda b, i: (b, i, 0)),
                  pl.BlockSpec((1, d), lambda b, i: (0, 0)),
                  pl.BlockSpec((None, 1, d), lambda b, i: (b, 0, 0)),
                  pl.BlockSpec((None, 1, d), lambda b, i: (b, 0, 0)),
                  pl.BlockSpec((d, LANE), lambda b, i: (0, 0))],
        out_specs=(pl.BlockSpec((None, tm, d), lambda b, i: (b, i, 0)),
                   pl.BlockSpec((None, tm, LANE), lambda b, i: (b, i, 0))),
        compiler_params=pltpu.CompilerParams(dimension_semantics=("parallel", "parallel"),
                                             vmem_limit_bytes=VMEM_LIMIT),
        name="norm",
    )(x, norm_g.reshape(1, d), scale[:, None, :], shift[:, None, :], w_small)


def _proj_kernel(h_ref, w_ref, o_ref):
    o_ref[...] = _dot(h_ref[...], w_ref[...]).astype(o_ref.dtype)


def _proj(h2d, w_wide, tm, tn):
    t, d = h2d.shape
    n = w_wide.shape[1]
    return pl.pallas_call(
        _proj_kernel,
        out_shape=jax.ShapeDtypeStruct((t, n), ACT_DTYPE),
        grid=(t // tm, n // tn),
        in_specs=[pl.BlockSpec((tm, d), lambda i, j: (i, 0)),
                  pl.BlockSpec((d, tn), lambda i, j: (0, j))],
        out_specs=pl.BlockSpec((tm, tn), lambda i, j: (i, j)),
        compiler_params=pltpu.CompilerParams(dimension_semantics=("parallel", "arbitrary"),
                                             vmem_limit_bytes=VMEM_LIMIT),
        name="proj",
    )(h2d, w_wide)


def _rope(y, cos_t, sin_t, first_half, half):
    fwd = pltpu.roll(y, LANE - half, 1)
    bwd = pltpu.roll(y, half, 1)
    return y * cos_t + jnp.where(first_half, fwd, bwd) * sin_t


def _prep_kernel(q_ref, k_ref, qi_ref, ki_ref, ca_ref, sa_ref, ci_ref, si_ref,
                 qg_ref, kg_ref, lg_ref, lb_ref, qo_ref, ko_ref, qio_ref, kio_ref):
    tm = q_ref.shape[0]
    lane = lax.broadcasted_iota(jnp.int32, (tm, LANE), 1)
    ca, sa = ca_ref[...], sa_ref[...]
    ci, si = ci_ref[...], si_ref[...]
    att_first = lane < ATT_ROT_DIM // 2
    idx_first = (lane % IDX_DIM) < IDX_ROT_DIM // 2

    def norm_rope(x, g):
        ms = jnp.mean(x * x, axis=-1, keepdims=True)
        y = x * lax.rsqrt(ms + EPS) * g
        return _rope(y, ca, sa, att_first, ATT_ROT_DIM // 2)

    q_scale = ATT_HEAD_DIM ** -0.5 * LOG2_E
    for h in range(ATT_HEADS):
        sl = slice(h * LANE, (h + 1) * LANE)
        qo_ref[:, sl] = (norm_rope(q_ref[:, sl].astype(F32), qg_ref[...]) * q_scale).astype(qo_ref.dtype)
    for h in range(ATT_KV_HEADS):
        sl = slice(h * LANE, (h + 1) * LANE)
        ko_ref[:, sl] = norm_rope(k_ref[:, sl].astype(F32), kg_ref[...]).astype(ko_ref.dtype)
    for j in range(IDX_Q_DIM // LANE):
        sl = slice(j * LANE, (j + 1) * LANE)
        y = _rope(qi_ref[:, sl].astype(F32), ci, si, idx_first, IDX_ROT_DIM // 2)
        qio_ref[:, sl] = (y * IDX_DIM ** -0.5).astype(qio_ref.dtype)
    x = ki_ref[...].astype(F32)
    real = lane < IDX_DIM
    mu = jnp.sum(jnp.where(real, x, 0.0), axis=-1, keepdims=True) * (1.0 / IDX_DIM)
    dlt = jnp.where(real, x - mu, 0.0)
    var = jnp.sum(dlt * dlt, axis=-1, keepdims=True) * (1.0 / IDX_DIM)
    y = dlt * lax.rsqrt(var + EPS) * lg_ref[...] + lb_ref[...]
    kio_ref[...] = _rope(y, ci, si, idx_first, IDX_ROT_DIM // 2).astype(kio_ref.dtype)


def _prep(proj, tabs, q_norm_g, k_norm_g, ln_g, ln_b, tm):
    t = proj.shape[0]
    ca, sa, ci, si = tabs
    row = lambda w, c0: pl.BlockSpec((tm, w), lambda i: (i, c0 // w))
    tab = pl.BlockSpec((tm, LANE), lambda i: (i, 0))
    par = pl.BlockSpec((1, LANE), lambda i: (0, 0))
    pad = lambda v: jnp.pad(v, (0, LANE - v.shape[0])).reshape(1, LANE)
    out = lambda w: pl.BlockSpec((tm, w), lambda i: (i, 0))
    return pl.pallas_call(
        _prep_kernel,
        out_shape=(jax.ShapeDtypeStruct((t, ATT_Q_DIM), MXU_DTYPE),
                   jax.ShapeDtypeStruct((t, ATT_KV_DIM), MXU_DTYPE),
                   jax.ShapeDtypeStruct((t, IDX_Q_DIM), MXU_DTYPE),
                   jax.ShapeDtypeStruct((t, LANE), MXU_DTYPE)),
        grid=(t // tm,),
        in_specs=[row(ATT_Q_DIM, COL_Q), row(ATT_KV_DIM, COL_K), row(IDX_Q_DIM, COL_QIDX), row(LANE, COL_KIDX),
                  tab, tab, tab, tab, par, par, par, par],
        out_specs=(out(ATT_Q_DIM), out(ATT_KV_DIM), out(IDX_Q_DIM), out(LANE)),
        compiler_params=pltpu.CompilerParams(dimension_semantics=("parallel",),
                                             vmem_limit_bytes=VMEM_LIMIT),
        name="prep",
    )(proj, proj, proj, proj, ca, sa, ci, si,
      q_norm_g.reshape(1, LANE), k_norm_g.reshape(1, LANE), pad(ln_g), pad(ln_b))


def _attn_kernel(qT_ref, qiT_ref, smT_ref, k_ref, vT_ref, ki_ref, o_ref,
                 keys_ref, m_ref, l_ref, acc_ref, kmax_ref, qcat_ref, *, tq, topk):
    i = pl.program_id(1)
    nchunk = i + 1
    tk = tq
    w_idx = smT_ref[SMALL_WIDX:SMALL_WIDX + IDX_HEADS, :] * (IDX_HEADS ** -0.5)
    qi_all = jnp.concatenate([qiT_ref[h * IDX_DIM:(h + 1) * IDX_DIM, :] for h in range(IDX_HEADS)], axis=1)
    row_iota = lax.broadcasted_iota(jnp.int32, (tk, tq), 0)
    col_iota = lax.broadcasted_iota(jnp.int32, (tk, tq), 1)

    def score_body(c, carry):
        r0 = pl.multiple_of(c * tk, tk)
        kc = ki_ref[pl.ds(r0, tk), :][:, :IDX_DIM]
        lg = _dot(kc, qi_all)
        s = w_idx[0:1, :] * jnp.maximum(lg[:, :tq], 0.0)
        for h in range(1, IDX_HEADS):
            s = s + w_idx[h:h + 1, :] * jnp.maximum(lg[:, h * tq:(h + 1) * tq], 0.0)
        bits = lax.bitcast_convert_type(s + 0.0, jnp.int32)
        key = jnp.where(bits < 0, bits ^ jnp.int32(0x7FFFFFFF), bits)
        causal = (r0 + row_iota) <= (i * tq + col_iota)
        keys_ref[pl.ds(r0, tk), :] = jnp.where(causal, key, jnp.int32(INT_MIN))
        return carry

    lax.fori_loop(0, nchunk, score_body, 0)

    def count(pred):
        def body(c, acc):
            kk = keys_ref[pl.ds(pl.multiple_of(c * tk, tk), tk), :]
            hit = jnp.where(pred(kk), jnp.int32(1), jnp.int32(0))
            return acc + hit.reshape(tk // SUBLANE, SUBLANE, tq).sum(axis=0)
        acc = lax.fori_loop(0, nchunk, body, jnp.zeros((SUBLANE, tq), jnp.int32))
        return acc.sum(axis=0, keepdims=True)

    def bit_body(j, ans):
        trial = ans + lax.shift_left(jnp.int32(1), 31 - j)
        cnt = count(lambda kk: kk >= trial)
        return jnp.where(cnt >= topk, trial, ans)

    ans = lax.fori_loop(0, 32, bit_body, jnp.full((1, tq), INT_MIN, jnp.int32))
    thr = jnp.maximum(ans, jnp.int32(INT_MIN + 1))
    cnt_gt = count(lambda kk: kk > thr)
    cnt_eq = count(lambda kk: kk == thr)
    need = topk - cnt_gt
    any_surplus_tie = jnp.max(jnp.where(cnt_eq > need, 1, 0)) > 0

    m_ref[...] = jnp.full(m_ref.shape, NEG_BIG, F32)
    l_ref[...] = jnp.zeros(l_ref.shape, F32)
    acc_ref[...] = jnp.zeros(acc_ref.shape, F32)

    @pl.when(i == 0)
    def _():
        for g in range(ATT_KV_HEADS):
            kg = k_ref[:, g * LANE:(g + 1) * LANE].astype(F32)
            n2 = jnp.max(jnp.sum(kg * kg, axis=-1, keepdims=True), axis=0, keepdims=True)
            kmax_ref[g:g + 1, :] = jnp.broadcast_to(n2, (1, LANE))

    bounded = True
    for h in range(ATT_HEADS):
        qh = qT_ref[h * LANE:(h + 1) * LANE, :].astype(F32)
        q2 = jnp.max(jnp.sum(qh * qh, axis=0, keepdims=True))
        k2 = jnp.max(kmax_ref[h // ATT_GROUP:h // ATT_GROUP + 1, :])
        bounded = jnp.logical_and(bounded, q2 * k2 <= SAFE_LOG2_LOGIT ** 2)
    plain = jnp.logical_and(bounded, jnp.logical_not(any_surplus_tie))

    def kv_chunk(c, g):
        r0 = pl.multiple_of(c * tk, tk)
        return (k_ref[pl.ds(r0, tk), g * LANE:(g + 1) * LANE],
                vT_ref[g * LANE:(g + 1) * LANE, pl.ds(r0, tk)])

    for g in range(ATT_KV_HEADS):
        qcat_ref[g] = jnp.concatenate(
            [qT_ref[h * LANE:(h + 1) * LANE, :] for h in range(g * ATT_GROUP, (g + 1) * ATT_GROUP)], axis=1)

    def masked_logits(c, g, bias):
        kc, vc = kv_chunk(c, g)
        s = _dot(kc, qcat_ref[g])
        return [s[:, r * tq:(r + 1) * tq] + bias for r in range(ATT_GROUP)], vc

    def plain_body(c, carry):
        kk = keys_ref[pl.ds(pl.multiple_of(c * tk, tk), tk), :]
        bias = jnp.where(kk >= thr, 0.0, NEG_BIG)
        for g in range(ATT_KV_HEADS):
            s, vc = masked_logits(c, g, bias)
            p = jnp.concatenate([jnp.exp2(sr) for sr in s], axis=1)
            l_ref[g] += jnp.sum(p, axis=0, keepdims=True)
            acc_ref[g] += _dot(vc, p.astype(MXU_DTYPE))
        return carry

    def general_body(c, taken):
        kk = keys_ref[pl.ds(pl.multiple_of(c * tk, tk), tk), :]
        eq = kk == thr
        eq_b = jnp.where(eq, 1.0, 0.0).astype(jnp.bfloat16)
        before = jnp.where(col_iota < row_iota, 1.0, 0.0).astype(jnp.bfloat16)
        rank = _dot(before, eq_b) + taken
        sel = (kk > thr) | (eq & (rank < need.astype(F32)))
        bias = jnp.where(sel, 0.0, NEG_BIG)
        for g in range(ATT_KV_HEADS):
            s, vc = masked_logits(c, g, bias)
            s = jnp.concatenate(s, axis=1)
            m_old = m_ref[g]
            m_new = jnp.maximum(m_old, jnp.max(s, axis=0, keepdims=True))
            alpha = jnp.exp2(m_old - m_new)
            p = jnp.exp2(s - m_new)
            l_ref[g] = alpha * l_ref[g] + jnp.sum(p, axis=0, keepdims=True)
            acc_ref[g] = alpha * acc_ref[g] + _dot(vc, p.astype(MXU_DTYPE))
            m_ref[g] = m_new
        return taken + jnp.sum(eq_b.astype(F32), axis=0, keepdims=True)

    @pl.when(plain)
    def _():
        lax.fori_loop(0, nchunk, plain_body, 0)

    @pl.when(jnp.logical_not(plain))
    def _():
        lax.fori_loop(0, nchunk, general_body, jnp.zeros((1, tq), F32))

    for h in range(ATT_HEADS):
        g, cs = h // ATT_GROUP, slice((h % ATT_GROUP) * tq, (h % ATT_GROUP + 1) * tq)
        o_ref[h * LANE:(h + 1) * LANE, :] = (acc_ref[g, :, cs] / l_ref[g, :, cs]).astype(o_ref.dtype)


def _attn(qT, qiT, smallT, k, vT, ki, tq):
    bsz, _, s = qT.shape
    topk = min(TOPK_MAX, s // 4)
    kern = functools.partial(_attn_kernel, tq=tq, topk=topk)
    qtile = lambda r: pl.BlockSpec((None, r, tq), lambda b, i: (b, 0, i))
    return pl.pallas_call(
        kern,
        out_shape=jax.ShapeDtypeStruct((bsz, ATT_Q_DIM, s), ACT_DTYPE),
        grid=(bsz, s // tq),
        in_specs=[qtile(ATT_Q_DIM), qtile(IDX_Q_DIM), qtile(LANE),
                  pl.BlockSpec((None, s, ATT_KV_DIM), lambda b, i: (b, 0, 0)),
                  pl.BlockSpec((None, ATT_KV_DIM, s), lambda b, i: (b, 0, 0)),
                  pl.BlockSpec((None, s, LANE), lambda b, i: (b, 0, 0))],
        out_specs=qtile(ATT_Q_DIM),
        scratch_shapes=[pltpu.VMEM((s, tq), jnp.int32),
                        pltpu.VMEM((ATT_KV_HEADS, 1, ATT_GROUP * tq), F32),
                        pltpu.VMEM((ATT_KV_HEADS, 1, ATT_GROUP * tq), F32),
                        pltpu.VMEM((ATT_KV_HEADS, ATT_HEAD_DIM, ATT_GROUP * tq), F32),
                        pltpu.VMEM((SUBLANE, LANE), F32),
                        pltpu.VMEM((ATT_KV_HEADS, ATT_HEAD_DIM, ATT_GROUP * tq), MXU_DTYPE)],
        compiler_params=pltpu.CompilerParams(dimension_semantics=("parallel", "arbitrary"),
                                             vmem_limit_bytes=VMEM_LIMIT),
        name="attn",
    )(qT, qiT, smallT, k, vT, ki)


def _ssm_kernel(xbc_ref, z_ref, sm_ref, smT_ref, cw_ref, cb_ref, dtb_ref, dtbT_ref, a_ref, aT_ref,
                dsk_ref, ng_ref, exp_ref, y_ref, ext_ref, st_ref, yacc_ref):
    ln = SSM_CHUNK
    c = pl.program_id(1)

    @pl.when(c == 0)
    def _():
        ext_ref[0:SUBLANE, :] = jnp.zeros((SUBLANE, SSM_CONV_DIM), F32)
        st_ref[...] = jnp.zeros(st_ref.shape, F32)

    raw = xbc_ref[...].astype(F32)
    ext_ref[SUBLANE:SUBLANE + ln, :] = raw
    acc = cb_ref[...] + cw_ref[0:1, :] * ext_ref[SUBLANE - 3:SUBLANE - 3 + ln, :]
    for t in range(1, SSM_CONV):
        acc = acc + cw_ref[t:t + 1, :] * ext_ref[SUBLANE - 3 + t:SUBLANE - 3 + t + ln, :]
    ext_ref[0:SUBLANE, :] = raw[ln - SUBLANE:, :]
    xbc = _silu(acc)
    xs = xbc[:, :SSM_D_INNER]
    b_all = xbc[:, SSM_D_INNER:SSM_D_INNER + SSM_GROUPS * SSM_STATE].astype(MXU_DTYPE)
    c_all = xbc[:, SSM_D_INNER + SSM_GROUPS * SSM_STATE:].astype(MXU_DTYPE)

    dt = _softplus(sm_ref[...] + dtb_ref[...])
    adt = dt * a_ref[...]
    r_i = lax.broadcasted_iota(jnp.int32, (ln, ln), 0)
    c_i = lax.broadcasted_iota(jnp.int32, (ln, ln), 1)
    tril = r_i >= c_i
    incl = jnp.where(tril, 1.0, 0.0).astype(jnp.bfloat16)
    a_col = _dot_sel_x(incl, adt)
    dtT = _softplus(smT_ref[...] + dtbT_ref[...])
    incl_t = jnp.where(r_i <= c_i, 1.0, 0.0).astype(jnp.bfloat16)
    a_row = _dot_x_sel(dtT * aT_ref[...], incl_t)

    expand = exp_ref[...]
    wide = _dot_x_sel(jnp.concatenate([a_col, dt], axis=0), expand)
    a_wide, dt_wide = wide[:ln], wide[ln:]
    a_last = a_wide[ln - 1:ln, :]
    xc = xs * dt_wide
    xc_b = xc.astype(MXU_DTYPE)
    xd_b = (xc * jnp.exp(a_last - a_wide)).astype(MXU_DTYPE)
    e_wide = jnp.exp(a_wide)
    chunk_decay = e_wide[ln - 1:ln, :]

    lane = lax.broadcasted_iota(jnp.int32, (ln, LANE), 1)
    low = lane < SSM_HEAD_DIM
    gw = SSM_D_INNER // SSM_GROUPS
    for g in range(SSM_GROUPS):
        gs = slice(g * gw, (g + 1) * gw)
        b_g = b_all[:, g * SSM_STATE:(g + 1) * SSM_STATE]
        c_g = c_all[:, g * SSM_STATE:(g + 1) * SSM_STATE]
        cb = lax.dot_general(c_g, b_g, (((1,), (1,)), ((), ())), preferred_element_type=F32)
        y_off = _dot(c_g, st_ref[:, gs].astype(MXU_DTYPE)) * e_wide[:, gs]
        st_ref[:, gs] = chunk_decay[:, gs] * st_ref[:, gs] + lax.dot_general(
            b_g, xd_b[:, gs], (((0,), (0,)), ((), ())), preferred_element_type=F32)
        yacc_ref[:, gs] = y_off
        for jp in range(gw // LANE):
            j = g * (gw // LANE) + jp
            xp = xc_b[:, j * LANE:(j + 1) * LANE]
            yd = None
            for half, keep in ((0, low), (1, jnp.logical_not(low))):
                hd = 2 * j + half
                diff = a_col[:, hd:hd + 1] - a_row[hd:hd + 1, :]
                lmat = jnp.exp(jnp.where(tril, diff, -jnp.inf))
                mh = (cb * lmat).astype(MXU_DTYPE)
                part = _dot(mh, jnp.where(keep, xp, jnp.zeros_like(xp)))
                yd = part if yd is None else yd + part
            yacc_ref[:, j * LANE:(j + 1) * LANE] += yd

    y = yacc_ref[...] + dsk_ref[...] * xs
    y = y * _silu(z_ref[...].astype(F32))
    for g in range(SSM_GROUPS):
        gs = slice(g * gw, (g + 1) * gw)
        yg = y[:, gs]
        ms = jnp.mean(yg * yg, axis=-1, keepdims=True)
        y_ref[:, gs] = (yg * lax.rsqrt(ms + EPS) * ng_ref[:, gs]).astype(y_ref.dtype)


def _ssm(proj3, small, smallT, conv_w, conv_b, dt_bias, a_log, d_skip, ssm_norm_g):
    bsz, s, _ = proj3.shape
    ln = SSM_CHUNK
    padl = lambda v: jnp.pad(v, (0, LANE - v.shape[0]))
    a = padl(-jnp.exp(a_log))
    dtb = padl(dt_bias)
    head_of = jnp.arange(SSM_D_INNER) // SSM_HEAD_DIM
    expand = (jnp.arange(LANE)[:, None] == head_of[None, :]).astype(jnp.bfloat16)
    full = lambda shp: pl.BlockSpec(shp, lambda b, c: (0,) * len(shp))
    return pl.pallas_call(
        _ssm_kernel,
        out_shape=jax.ShapeDtypeStruct((bsz, s, SSM_D_INNER), ACT_DTYPE),
        grid=(bsz, s // ln),
        in_specs=[pl.BlockSpec((None, ln, SSM_CONV_DIM), lambda b, c: (b, c, COL_XBC // SSM_CONV_DIM)),
                  pl.BlockSpec((None, ln, SSM_D_INNER), lambda b, c: (b, c, COL_ZSSM // SSM_D_INNER)),
                  pl.BlockSpec((None, ln, LANE), lambda b, c: (b, c, 0)),
                  pl.BlockSpec((None, LANE, ln), lambda b, c: (b, 0, c)),
                  full((SSM_CONV, SSM_CONV_DIM)), full((1, SSM_CONV_DIM)),
                  full((1, LANE)), full((LANE, 1)), full((1, LANE)), full((LANE, 1)),
                  full((1, SSM_D_INNER)), full((1, SSM_D_INNER)), full((LANE, SSM_D_INNER))],
        out_specs=pl.BlockSpec((None, ln, SSM_D_INNER), lambda b, c: (b, c, 0)),
        scratch_shapes=[pltpu.VMEM((SUBLANE + ln, SSM_CONV_DIM), F32),
                        pltpu.VMEM((SSM_STATE, SSM_D_INNER), F32),
                        pltpu.VMEM((ln, SSM_D_INNER), F32)],
        compiler_params=pltpu.CompilerParams(dimension_semantics=("parallel", "arbitrary"),
                                             vmem_limit_bytes=VMEM_LIMIT),
        name="ssm",
    )(proj3, proj3, small, smallT, conv_w, conv_b.reshape(1, -1),
      dtb.reshape(1, LANE), dtb.reshape(LANE, 1), a.reshape(1, LANE), a.reshape(LANE, 1),
      jnp.repeat(d_skip, SSM_HEAD_DIM).reshape(1, -1), ssm_norm_g.reshape(1, -1), expand)


def _final_kernel(o_ref, za_ref, y_ref, ga_ref, gs_ref, x_ref, gate_ref, wa_ref, ws_ref, wo_ref, out_ref):
    o = (o_ref[...].astype(F32) * _silu(za_ref[...].astype(F32))).astype(MXU_DTYPE)
    y_att = _dot(o, wa_ref[...])
    y_ssm = _dot(y_ref[...], ws_ref[...])
    merged = _sigmoid(ga_ref[...].astype(F32)) * y_att + _sigmoid(gs_ref[...].astype(F32)) * y_ssm
    out_ref[...] = x_ref[...] + gate_ref[...] * _dot(merged.astype(MXU_DTYPE), wo_ref[...])


def _final(o3, proj3, y3, x, gate, w_att, w_ssm, w_out, tm):
    bsz, s, d = x.shape
    tile = lambda w, c0: pl.BlockSpec((None, tm, w), lambda b, i: (b, i, c0 // w))
    full = lambda shp: pl.BlockSpec(shp, lambda b, i: (0,) * len(shp))
    return pl.pallas_call(
        _final_kernel,
        out_shape=jax.ShapeDtypeStruct((bsz, s, d), x.dtype),
        grid=(bsz, s // tm),
        in_specs=[tile(ATT_Q_DIM, 0), tile(ATT_Q_DIM, COL_ZATT), tile(SSM_D_INNER, 0),
                  tile(D_MODEL, COL_GATE), tile(D_MODEL, COL_GATE + D_MODEL), tile(d, 0),
                  pl.BlockSpec((None, 1, d), lambda b, i: (b, 0, 0)),
                  full((ATT_Q_DIM, d)), full((SSM_D_INNER, d)), full((d, d))],
        out_specs=tile(d, 0),
        compiler_params=pltpu.CompilerParams(dimension_semantics=("parallel", "parallel"),
                                             vmem_limit_bytes=VMEM_LIMIT),
        name="final",
    )(o3, proj3, y3, proj3, proj3, x, gate[:, None, :],
      w_att.astype(MXU_DTYPE), w_ssm.astype(MXU_DTYPE), w_out.astype(MXU_DTYPE))


def _permute_w_in(w_in):
    sizes = (ATT_Q_DIM, ATT_KV_DIM, ATT_KV_DIM, ATT_Q_DIM, IDX_Q_DIM, IDX_DIM, IDX_HEADS,
             SSM_D_INNER, SSM_CONV_DIM, SSM_HEADS, 2 * D_MODEL)
    offs = [0]
    for sz in sizes:
        offs.append(offs[-1] + sz)
    q, k, v, z_att, q_idx, k_idx, w_idx, z_ssm, xbc, dt, gate = (
        w_in[:, offs[n]:offs[n + 1]] for n in range(len(sizes)))
    d = w_in.shape[0]
    zeros = lambda n: jnp.zeros((d, n), w_in.dtype)
    wide = jnp.concatenate(
        [xbc, q, z_ssm, gate, z_att, k, v, q_idx, k_idx, zeros(LANE - IDX_DIM),
         zeros(WIDE_COLS - COL_KIDX - LANE)], axis=1)
    small = jnp.concatenate([dt, w_idx, zeros(LANE - SSM_HEADS - IDX_HEADS)], axis=1)
    return wide.astype(MXU_DTYPE), small


def _rope_tables(positions):
    def angles(rot):
        inv = jnp.power(ROPE_THETA, -(jnp.arange(0, rot, 2, dtype=F32) / rot))
        ang = positions.astype(F32)[..., None] * inv
        return jnp.cos(ang), jnp.sin(ang)

    def lanes(cos, sin, width):
        half = cos.shape[-1]
        lead = cos.shape[:-1]
        ones = jnp.ones(lead + (width - 2 * half,), F32)
        ct = jnp.concatenate([cos, cos, ones], axis=-1)
        st = jnp.concatenate([-sin, sin, 0.0 * ones], axis=-1)
        reps = LANE // width
        return (jnp.tile(ct, reps).reshape(-1, LANE), jnp.tile(st, reps).reshape(-1, LANE))

    ca, sa = lanes(*angles(ATT_ROT_DIM), ATT_HEAD_DIM)
    ci, si = lanes(*angles(IDX_ROT_DIM), IDX_DIM)
    return ca, sa, ci, si


def _tile(n, pref):
    return pref if n % pref == 0 else n


def kernel(x, c, positions, ada_w, ada_b, norm_g, w_in, q_norm_g, k_norm_g, idx_k_ln_g, idx_k_ln_b,
           conv_w, conv_b, dt_bias, a_log, d_skip, ssm_norm_g, w_branch_att, w_branch_ssm, w_out):
    bsz, s, d = x.shape
    t = bsz * s
    tabs = _rope_tables(positions)
    for l in range(ada_w.shape[0]):
        mod = _ada(c, ada_w[l], ada_b[l])
        shift, scale, gate = mod[:, :d], mod[:, d:2 * d], mod[:, 2 * d:]
        w_wide, w_small = _permute_w_in(w_in[l])
        h, small = _norm(x, norm_g[l], scale, shift, w_small, _tile(s, 512))
        proj = _proj(h.reshape(t, d), w_wide, _tile(t, 1024), 1024)
        proj3 = proj.reshape(bsz, s, WIDE_COLS)

        q, k, q_idx, k_idx = _prep(proj, tabs, q_norm_g[l], k_norm_g[l], idx_k_ln_g[l], idx_k_ln_b[l],
                                   _tile(t, 512))
        to_t = lambda a: jnp.swapaxes(a.reshape(bsz, s, -1), 1, 2)
        smallT = jnp.swapaxes(small, 1, 2)
        oT = _attn(to_t(q), to_t(q_idx), smallT, k.reshape(bsz, s, -1),
                   jnp.swapaxes(proj3[:, :, COL_V:COL_V + ATT_KV_DIM], 1, 2),
                   k_idx.reshape(bsz, s, LANE), _tile(s, 256))
        o3 = jnp.swapaxes(oT, 1, 2)

        y3 = _ssm(proj3, small, smallT, conv_w[l], conv_b[l], dt_bias[l], a_log[l], d_skip[l], ssm_norm_g[l])
        x = _final(o3, proj3, y3, x, gate, w_branch_att[l], w_branch_ssm[l], w_out[l], _tile(s, 512))
    return x
```

```python
import functools

import jax
import jax.numpy as jnp
from jax import lax
from jax.experimental import pallas as pl
from jax.experimental.pallas import tpu as pltpu

F32 = jnp.float32
MXU_DTYPE = jnp.bfloat16
ACT_DTYPE = jnp.bfloat16

D_MODEL = 1024
ATT_HEADS = 8
ATT_KV_HEADS = 2
ATT_HEAD_DIM = 128
ATT_GROUP = ATT_HEADS // ATT_KV_HEADS
ROPE_THETA = 500000.0
ATT_ROT_DIM = 32
IDX_HEADS = 4
IDX_DIM = 64
IDX_ROT_DIM = 16
TOPK_MAX = 256
SSM_D_INNER = 2048
SSM_HEAD_DIM = 64
SSM_HEADS = 32
SSM_GROUPS = 4
SSM_STATE = 128
SSM_CONV = 4
SSM_CHUNK = 128
SSM_CONV_DIM = SSM_D_INNER + 2 * SSM_GROUPS * SSM_STATE
EPS = 1e-6

ATT_Q_DIM = ATT_HEADS * ATT_HEAD_DIM
ATT_KV_DIM = ATT_KV_HEADS * ATT_HEAD_DIM
IDX_Q_DIM = IDX_HEADS * IDX_DIM
LANE = 128
SUBLANE = 8
INT_MIN = -(2 ** 31)
HALF_BIAS = 2 ** 15
NEG_BIG = -1e30
LOG2_E = 1.4426950408889634
SAFE_LOG2_LOGIT = 60.0
VMEM_LIMIT = 56 * 1024 * 1024

COL_XBC = 0
COL_Q = COL_XBC + SSM_CONV_DIM
COL_ZSSM = COL_Q + ATT_Q_DIM
COL_GATE = COL_ZSSM + SSM_D_INNER
COL_ZATT = COL_GATE + 2 * D_MODEL
COL_K = COL_ZATT + ATT_Q_DIM
COL_V = COL_K + ATT_KV_DIM
COL_QIDX = COL_V + ATT_KV_DIM
COL_KIDX = COL_QIDX + IDX_Q_DIM
WIDE_COLS = 10240
SMALL_DT = 0
SMALL_WIDX = SSM_HEADS


def _split3(x):
    hi = x.astype(jnp.bfloat16)
    r1 = x - hi.astype(F32)
    mid = r1.astype(jnp.bfloat16)
    lo = (r1 - mid.astype(F32)).astype(jnp.bfloat16)
    return hi, mid, lo


def _dot(a, b):
    return jnp.dot(a, b, preferred_element_type=F32)


def _dot_f32(a, b):
    a0, a1, a2 = _split3(a)
    b0, b1, b2 = _split3(b)
    return (_dot(a0, b0) + (_dot(a0, b1) + _dot(a1, b0))
            + (_dot(a0, b2) + _dot(a1, b1) + _dot(a2, b0)))


def _dot_x_sel(x, sel):
    m = x.shape[0]
    parts = jnp.concatenate(_split3(x), axis=0)
    r = _dot(parts, sel)
    return r[:m] + r[m:2 * m] + r[2 * m:]


def _dot_sel_x(sel, x):
    hi, mid, lo = _split3(x)
    return _dot(sel, hi) + _dot(sel, mid) + _dot(sel, lo)


def _sigmoid(x):
    return 1.0 / (1.0 + jnp.exp(-x))


def _silu(x):
    return x * _sigmoid(x)


def _softplus(x):
    return jnp.maximum(x, 0.0) + jnp.log(1.0 + jnp.exp(-jnp.abs(x)))


def _ada_kernel(c_ref, w_ref, b_ref, o_ref):
    o_ref[...] = _dot_f32(_silu(c_ref[...]), w_ref[...]) + b_ref[...]


def _ada(c, ada_w, ada_b):
    bsz = c.shape[0]
    return pl.pallas_call(
        _ada_kernel,
        out_shape=jax.ShapeDtypeStruct((bsz, 3 * D_MODEL), F32),
        grid=(3,),
        in_specs=[pl.BlockSpec((bsz, D_MODEL), lambda j: (0, 0)),
                  pl.BlockSpec((D_MODEL, D_MODEL), lambda j: (0, j)),
                  pl.BlockSpec((1, D_MODEL), lambda j: (0, j))],
        out_specs=pl.BlockSpec((bsz, D_MODEL), lambda j: (0, j)),
        compiler_params=pltpu.CompilerParams(dimension_semantics=("arbitrary",),
                                             vmem_limit_bytes=VMEM_LIMIT),
        name="ada",
    )(c, ada_w, ada_b.reshape(1, -1))


def _norm_kernel(x_ref, g_ref, scale_ref, shift_ref, wsm_ref, h_ref, small_ref):
    x = x_ref[...]
    ms = jnp.mean(x * x, axis=-1, keepdims=True)
    y = x * lax.rsqrt(ms + EPS) * g_ref[...]
    h = y * (1.0 + scale_ref[...]) + shift_ref[...]
    h_ref[...] = h.astype(h_ref.dtype)
    small_ref[...] = _dot_f32(h, wsm_ref[...])


def _norm(x, norm_g, scale, shift, w_small, tm):
    bsz, s, d = x.shape
    return pl.pallas_call(
        _norm_kernel,
        out_shape=(jax.ShapeDtypeStruct((bsz, s, d), ACT_DTYPE),
                   jax.ShapeDtypeStruct((bsz, s, LANE), F32)),
        grid=(bsz, s // tm),
        in_specs=[pl.BlockSpec((None, tm, d), lambda b, i: (b, i, 0)),
                  pl.BlockSpec((1, d), lambda b, i: (0, 0)),
                  pl.BlockSpec((None, 1, d), lambda b, i: (b, 0, 0)),
                  pl.BlockSpec((None, 1, d), lambda b, i: (b, 0, 0)),
                  pl.BlockSpec((d, LANE), lambda b, i: (0, 0))],
        out_specs=(pl.BlockSpec((None, tm, d), lambda b, i: (b, i, 0)),
                   pl.BlockSpec((None, tm, LANE), lambda b, i: (b, i, 0))),
        compiler_params=pltpu.CompilerParams(dimension_semantics=("parallel", "parallel"),
                                             vmem_limit_bytes=VMEM_LIMIT),
        name="norm",
    )(x, norm_g.reshape(1, d), scale[:, None, :], shift[:, None, :], w_small)


def _proj_kernel(h_ref, w_ref, o_ref):
    o_ref[...] = _dot(h_ref[...], w_ref[...]).astype(o_ref.dtype)


def _proj(h2d, w_wide, tm, tn):
    t, d = h2d.shape
    n = w_wide.shape[1]
    return pl.pallas_call(
        _proj_kernel,
        out_shape=jax.ShapeDtypeStruct((t, n), ACT_DTYPE),
        grid=(t // tm, n // tn),
        in_specs=[pl.BlockSpec((tm, d), lambda i, j: (i, 0)),
                  pl.BlockSpec((d, tn), lambda i, j: (0, j))],
        out_specs=pl.BlockSpec((tm, tn), lambda i, j: (i, j)),
        compiler_params=pltpu.CompilerParams(dimension_semantics=("parallel", "arbitrary"),
                                             vmem_limit_bytes=VMEM_LIMIT),
        name="proj",
    )(h2d, w_wide)


def _rope(y, cos_t, sin_t, first_half, half):
    fwd = pltpu.roll(y, LANE - half, 1)
    bwd = pltpu.roll(y, half, 1)
    return y * cos_t + jnp.where(first_half, fwd, bwd) * sin_t


def _prep_kernel(q_ref, k_ref, qi_ref, ki_ref, ca_ref, sa_ref, ci_ref, si_ref,
                 qg_ref, kg_ref, lg_ref, lb_ref, qo_ref, ko_ref, qio_ref, kio_ref):
    tm = q_ref.shape[0]
    lane = lax.broadcasted_iota(jnp.int32, (tm, LANE), 1)
    ca, sa = ca_ref[...], sa_ref[...]
    ci, si = ci_ref[...], si_ref[...]
    att_first = lane < ATT_ROT_DIM // 2
    idx_first = (lane % IDX_DIM) < IDX_ROT_DIM // 2

    def norm_rope(x, g):
        ms = jnp.mean(x * x, axis=-1, keepdims=True)
        y = x * lax.rsqrt(ms + EPS) * g
        return _rope(y, ca, sa, att_first, ATT_ROT_DIM // 2)

    q_scale = ATT_HEAD_DIM ** -0.5 * LOG2_E
    for h in range(ATT_HEADS):
        sl = slice(h * LANE, (h + 1) * LANE)
        qo_ref[:, sl] = (norm_rope(q_ref[:, sl].astype(F32), qg_ref[...]) * q_scale).astype(qo_ref.dtype)
    for h in range(ATT_KV_HEADS):
        sl = slice(h * LANE, (h + 1) * LANE)
        ko_ref[:, sl] = norm_rope(k_ref[:, sl].astype(F32), kg_ref[...]).astype(ko_ref.dtype)
    for j in range(IDX_Q_DIM // LANE):
        sl = slice(j * LANE, (j + 1) * LANE)
        y = _rope(qi_ref[:, sl].astype(F32), ci, si, idx_first, IDX_ROT_DIM // 2)
        qio_ref[:, sl] = (y * IDX_DIM ** -0.5).astype(qio_ref.dtype)
    x = ki_ref[...].astype(F32)
    real = lane < IDX_DIM
    mu = jnp.sum(jnp.where(real, x, 0.0), axis=-1, keepdims=True) * (1.0 / IDX_DIM)
    dlt = jnp.where(real, x - mu, 0.0)
    var = jnp.sum(dlt * dlt, axis=-1, keepdims=True) * (1.0 / IDX_DIM)
    y = dlt * lax.rsqrt(var + EPS) * lg_ref[...] + lb_ref[...]
    kio_ref[...] = _rope(y, ci, si, idx_first, IDX_ROT_DIM // 2).astype(kio_ref.dtype)


def _prep(proj, tabs, q_norm_g, k_norm_g, ln_g, ln_b, tm):
    t = proj.shape[0]
    ca, sa, ci, si = tabs
    row = lambda w, c0: pl.BlockSpec((tm, w), lambda i: (i, c0 // w))
    tab = pl.BlockSpec((tm, LANE), lambda i: (i, 0))
    par = pl.BlockSpec((1, LANE), lambda i: (0, 0))
    pad = lambda v: jnp.pad(v, (0, LANE - v.shape[0])).reshape(1, LANE)
    out = lambda w: pl.BlockSpec((tm, w), lambda i: (i, 0))
    return pl.pallas_call(
        _prep_kernel,
        out_shape=(jax.ShapeDtypeStruct((t, ATT_Q_DIM), MXU_DTYPE),
                   jax.ShapeDtypeStruct((t, ATT_KV_DIM), MXU_DTYPE),
                   jax.ShapeDtypeStruct((t, IDX_Q_DIM), MXU_DTYPE),
                   jax.ShapeDtypeStruct((t, LANE), MXU_DTYPE)),
        grid=(t // tm,),
        in_specs=[row(ATT_Q_DIM, COL_Q), row(ATT_KV_DIM, COL_K), row(IDX_Q_DIM, COL_QIDX), row(LANE, COL_KIDX),
                  tab, tab, tab, tab, par, par, par, par],
        out_specs=(out(ATT_Q_DIM), out(ATT_KV_DIM), out(IDX_Q_DIM), out(LANE)),
        compiler_params=pltpu.CompilerParams(dimension_semantics=("parallel",),
                                             vmem_limit_bytes=VMEM_LIMIT),
        name="prep",
    )(proj, proj, proj, proj, ca, sa, ci, si,
      q_norm_g.reshape(1, LANE), k_norm_g.reshape(1, LANE), pad(ln_g), pad(ln_b))


def _attn_kernel(q_ref, qiT_ref, smT_ref, kT_ref, v_ref, ki_ref, qg_ref, kg_ref, o_ref,
                 keys_ref, hi_ref, lo_ref, m_ref, acc_ref, qcat_ref, *, tq, topk):
    i = pl.program_id(1)
    nchunk = i + 1
    tk = tq
    w_idx = smT_ref[SMALL_WIDX:SMALL_WIDX + IDX_HEADS, :] * (IDX_HEADS ** -0.5)
    qi_all = jnp.concatenate([qiT_ref[h * IDX_DIM:(h + 1) * IDX_DIM, :] for h in range(IDX_HEADS)], axis=1)
    row_iota = lax.broadcasted_iota(jnp.int32, (tk, tq), 0)
    col_iota = lax.broadcasted_iota(jnp.int32, (tk, tq), 1)

    def score_chunk(c, diagonal):
        r0 = pl.multiple_of(c * tk, tk)
        kc = ki_ref[pl.ds(r0, tk), :][:, :IDX_DIM]
        lg = _dot(kc, qi_all)
        s = w_idx[0:1, :] * jnp.maximum(lg[:, :tq], 0.0)
        for h in range(1, IDX_HEADS):
            s = s + w_idx[h:h + 1, :] * jnp.maximum(lg[:, h * tq:(h + 1) * tq], 0.0)
        bits = lax.bitcast_convert_type(s + 0.0, jnp.int32)
        key = jnp.where(bits < 0, bits ^ jnp.int32(0x7FFFFFFF), bits)
        if diagonal:
            key = jnp.where(row_iota <= col_iota, key, jnp.int32(INT_MIN))
        keys_ref[pl.ds(r0, tk), :] = key
        hi_ref[pl.ds(r0, tk), :] = (key >> 16).astype(jnp.int16)
        lo_ref[pl.ds(r0, tk), :] = ((key & 0xFFFF) - HALF_BIAS).astype(jnp.int16)

    def score_body(c, carry):
        score_chunk(c, False)
        return carry

    lax.fori_loop(0, nchunk - 1, score_body, 0)
    score_chunk(nchunk - 1, True)

    def count(pred):
        def body(c, acc):
            kk = keys_ref[pl.ds(pl.multiple_of(c * tk, tk), tk), :]
            hit = jnp.where(pred(kk), jnp.int32(1), jnp.int32(0))
            return acc + hit.reshape(tk // SUBLANE, SUBLANE, tq).sum(axis=0)
        acc = lax.fori_loop(0, nchunk, body, jnp.zeros((SUBLANE, tq), jnp.int32))
        return acc.sum(axis=0, keepdims=True)

    pack = 2 * SUBLANE

    def count16(ref, pred):
        def body(c, acc):
            kk = ref[pl.ds(pl.multiple_of(c * tk, tk), tk), :]
            hit = jnp.where(pred(kk), jnp.bfloat16(1), jnp.bfloat16(0))
            parts = [hit[r * pack:(r + 1) * pack, :] for r in range(tk // pack)]
            while len(parts) > 1:
                parts = [parts[n] + parts[n + 1] for n in range(0, len(parts), 2)]
            return acc + parts[0].astype(F32)
        acc = lax.fori_loop(0, nchunk, body, jnp.zeros((pack, tq), F32))
        return acc.sum(axis=0, keepdims=True)

    def bisect16(ref, want):
        def bit_body(j, ans):
            trial = ans + lax.shift_left(jnp.int32(1), 15 - j)
            t16 = trial.astype(jnp.int16)
            cnt = count16(ref, lambda kk: kk >= t16)
            return jnp.where(cnt >= want, trial, ans)
        return lax.fori_loop(0, 16, bit_body, jnp.full((1, tq), -HALF_BIAS, jnp.int32))

    ans_hi = bisect16(hi_ref, float(topk))
    hi16 = ans_hi.astype(jnp.int16)
    want_lo = float(topk) - count16(hi_ref, lambda kk: kk > hi16)

    def restrict_body(c, carry):
        sl = pl.ds(pl.multiple_of(c * tk, tk), tk)
        lo_ref[sl, :] = jnp.where(hi_ref[sl, :] == hi16, lo_ref[sl, :], jnp.int16(-HALF_BIAS))
        return carry

    lax.fori_loop(0, nchunk, restrict_body, 0)
    ans_lo = bisect16(lo_ref, want_lo)
    ans = lax.shift_left(ans_hi, 16) | (ans_lo + HALF_BIAS)
    thr = jnp.maximum(ans, jnp.int32(INT_MIN + 1))
    cnt_gt = count(lambda kk: kk > thr)
    cnt_eq = count(lambda kk: kk == thr)
    need = topk - cnt_gt
    any_surplus_tie = jnp.max(jnp.where(cnt_eq > need, 1, 0)) > 0

    for g in range(ATT_KV_HEADS):
        qcat_ref[g] = jnp.concatenate(
            [q_ref[:, h * LANE:(h + 1) * LANE] for h in range(g * ATT_GROUP, (g + 1) * ATT_GROUP)], axis=0)
    acc_ref[...] = jnp.zeros(acc_ref.shape, F32)
    ones = jnp.ones((tk, LANE), MXU_DTYPE)

    bound = (ATT_HEAD_DIM ** 0.5 * LOG2_E) * jnp.max(jnp.abs(qg_ref[...])) * jnp.max(jnp.abs(kg_ref[...]))
    plain = jnp.logical_and(bound <= SAFE_LOG2_LOGIT, jnp.logical_not(any_surplus_tie))

    def masked_logits(c, g, bias_t):
        r0 = pl.multiple_of(c * tk, tk)
        s = _dot(qcat_ref[g], kT_ref[g * LANE:(g + 1) * LANE, pl.ds(r0, tk)])
        vx = jnp.concatenate([v_ref[pl.ds(r0, tk), g * LANE:(g + 1) * LANE], ones], axis=1)
        return [s[r * tq:(r + 1) * tq, :] + bias_t for r in range(ATT_GROUP)], vx

    def plain_body(c, carry):
        kk = keys_ref[pl.ds(pl.multiple_of(c * tk, tk), tk), :]
        bias_t = jnp.where(kk >= thr, 0.0, NEG_BIG).T
        for g in range(ATT_KV_HEADS):
            s, vx = masked_logits(c, g, bias_t)
            p = jnp.concatenate([jnp.exp2(sr).astype(MXU_DTYPE) for sr in s], axis=0)
            acc_ref[g] += _dot(p, vx)
        return carry

    def general_body(c, taken):
        kk = keys_ref[pl.ds(pl.multiple_of(c * tk, tk), tk), :]
        eq = kk == thr
        eq_b = jnp.where(eq, 1.0, 0.0).astype(jnp.bfloat16)
        before = jnp.where(col_iota < row_iota, 1.0, 0.0).astype(jnp.bfloat16)
        rank = _dot(before, eq_b) + taken
        sel = (kk > thr) | (eq & (rank < need.astype(F32)))
        bias_t = jnp.where(sel, 0.0, NEG_BIG).T
        for g in range(ATT_KV_HEADS):
            s, vx = masked_logits(c, g, bias_t)
            s = jnp.concatenate(s, axis=0)
            m_old = m_ref[g]
            m_new = jnp.maximum(m_old, jnp.max(s, axis=1, keepdims=True))
            p = jnp.exp2(s - m_new)
            acc_ref[g] = jnp.exp2(m_old - m_new) * acc_ref[g] + _dot(p.astype(MXU_DTYPE), vx)
            m_ref[g] = m_new
        return taken + jnp.sum(eq_b.astype(F32), axis=0, keepdims=True)

    @pl.when(plain)
    def _():
        lax.fori_loop(0, nchunk, plain_body, 0)

    @pl.when(jnp.logical_not(plain))
    def _():
        m_ref[...] = jnp.full(m_ref.shape, NEG_BIG, F32)
        lax.fori_loop(0, nchunk, general_body, jnp.zeros((1, tq), F32))

    for h in range(ATT_HEADS):
        g, rs = h // ATT_GROUP, slice((h % ATT_GROUP) * tq, (h % ATT_GROUP + 1) * tq)
        o_ref[:, h * LANE:(h + 1) * LANE] = (acc_ref[g, rs, :LANE] / acc_ref[g, rs, LANE:]).astype(o_ref.dtype)


def _attn(q, qiT, smallT, kT, proj3, ki, q_norm_g, k_norm_g, tq):
    bsz, s, _ = q.shape
    topk = min(TOPK_MAX, s // 4)
    kern = functools.partial(_attn_kernel, tq=tq, topk=topk)
    qtile_t = lambda r: pl.BlockSpec((None, r, tq), lambda b, i: (b, 0, i))
    par = pl.BlockSpec((1, LANE), lambda b, i: (0, 0))
    return pl.pallas_call(
        kern,
        out_shape=jax.ShapeDtypeStruct((bsz, s, ATT_Q_DIM), ACT_DTYPE),
        grid=(bsz, s // tq),
        in_specs=[pl.BlockSpec((None, tq, ATT_Q_DIM), lambda b, i: (b, i, 0)),
                  qtile_t(IDX_Q_DIM), qtile_t(LANE),
                  pl.BlockSpec((None, ATT_KV_DIM, s), lambda b, i: (b, 0, 0)),
                  pl.BlockSpec((None, s, ATT_KV_DIM), lambda b, i: (b, 0, COL_V // ATT_KV_DIM)),
                  pl.BlockSpec((None, s, LANE), lambda b, i: (b, 0, 0)), par, par],
        out_specs=pl.BlockSpec((None, tq, ATT_Q_DIM), lambda b, i: (b, i, 0)),
        scratch_shapes=[pltpu.VMEM((s, tq), jnp.int32),
                        pltpu.VMEM((s, tq), jnp.int16),
                        pltpu.VMEM((s, tq), jnp.int16),
                        pltpu.VMEM((ATT_KV_HEADS, ATT_GROUP * tq, 1), F32),
                        pltpu.VMEM((ATT_KV_HEADS, ATT_GROUP * tq, 2 * LANE), F32),
                        pltpu.VMEM((ATT_KV_HEADS, ATT_GROUP * tq, ATT_HEAD_DIM), MXU_DTYPE)],
        compiler_params=pltpu.CompilerParams(dimension_semantics=("parallel", "arbitrary"),
                                             vmem_limit_bytes=VMEM_LIMIT),
        name="attn",
    )(q, qiT, smallT, kT, proj3, ki, q_norm_g.reshape(1, LANE), k_norm_g.reshape(1, LANE))


def _ssm_kernel(xbc_ref, z_ref, sm_ref, smT_ref, cw_ref, cb_ref, dtb_ref, dtbT_ref, a_ref, aT_ref,
                dsk_ref, ng_ref, exp_ref, y_ref, ext_ref, st_ref, yacc_ref):
    ln = SSM_CHUNK
    c = pl.program_id(1)

    @pl.when(c == 0)
    def _():
        ext_ref[0:SUBLANE, :] = jnp.zeros((SUBLANE, SSM_CONV_DIM), F32)
        st_ref[...] = jnp.zeros(st_ref.shape, F32)

    raw = xbc_ref[...].astype(F32)
    ext_ref[SUBLANE:SUBLANE + ln, :] = raw
    acc = cb_ref[...] + cw_ref[0:1, :] * ext_ref[SUBLANE - 3:SUBLANE - 3 + ln, :]
    for t in range(1, SSM_CONV):
        acc = acc + cw_ref[t:t + 1, :] * ext_ref[SUBLANE - 3 + t:SUBLANE - 3 + t + ln, :]
    ext_ref[0:SUBLANE, :] = raw[ln - SUBLANE:, :]
    xbc = _silu(acc)
    xs = xbc[:, :SSM_D_INNER]
    b_all = xbc[:, SSM_D_INNER:SSM_D_INNER + SSM_GROUPS * SSM_STATE].astype(MXU_DTYPE)
    c_all = xbc[:, SSM_D_INNER + SSM_GROUPS * SSM_STATE:].astype(MXU_DTYPE)

    dt = _softplus(sm_ref[...] + dtb_ref[...])
    adt = dt * a_ref[...]
    r_i = lax.broadcasted_iota(jnp.int32, (ln, ln), 0)
    c_i = lax.broadcasted_iota(jnp.int32, (ln, ln), 1)
    tril = r_i >= c_i
    incl = jnp.where(tril, 1.0, 0.0).astype(jnp.bfloat16)
    a_col = _dot_sel_x(incl, adt)
    dtT = _softplus(smT_ref[...] + dtbT_ref[...])
    incl_t = jnp.where(r_i <= c_i, 1.0, 0.0).astype(jnp.bfloat16)
    a_row = _dot_x_sel(dtT * aT_ref[...], incl_t)

    expand = exp_ref[...]
    wide = _dot_x_sel(jnp.concatenate([a_col, dt], axis=0), expand)
    a_wide, dt_wide = wide[:ln], wide[ln:]
    a_last = a_wide[ln - 1:ln, :]
    xc = xs * dt_wide
    xc_b = xc.astype(MXU_DTYPE)
    xd_b = (xc * jnp.exp(a_last - a_wide)).astype(MXU_DTYPE)
    e_wide = jnp.exp(a_wide)
    chunk_decay = e_wide[ln - 1:ln, :]

    lane = lax.broadcasted_iota(jnp.int32, (ln, LANE), 1)
    low = lane < SSM_HEAD_DIM
    gw = SSM_D_INNER // SSM_GROUPS
    for g in range(SSM_GROUPS):
        gs = slice(g * gw, (g + 1) * gw)
        b_g = b_all[:, g * SSM_STATE:(g + 1) * SSM_STATE]
        c_g = c_all[:, g * SSM_STATE:(g + 1) * SSM_STATE]
        cb = lax.dot_general(c_g, b_g, (((1,), (1,)), ((), ())), preferred_element_type=F32)
        y_off = _dot(c_g, st_ref[:, gs].astype(MXU_DTYPE)) * e_wide[:, gs]
        st_ref[:, gs] = chunk_decay[:, gs] * st_ref[:, gs] + lax.dot_general(
            b_g, xd_b[:, gs], (((0,), (0,)), ((), ())), preferred_element_type=F32)
        yacc_ref[:, gs] = y_off
        for jp in range(gw // LANE):
            j = g * (gw // LANE) + jp
            xp = xc_b[:, j * LANE:(j + 1) * LANE]
            yd = None
            for half, keep in ((0, low), (1, jnp.logical_not(low))):
                hd = 2 * j + half
                diff = a_col[:, hd:hd + 1] - a_row[hd:hd + 1, :]
                lmat = jnp.exp(jnp.where(tril, diff, -jnp.inf))
                mh = (cb * lmat).astype(MXU_DTYPE)
                part = _dot(mh, jnp.where(keep, xp, jnp.zeros_like(xp)))
                yd = part if yd is None else yd + part
            yacc_ref[:, j * LANE:(j + 1) * LANE] += yd

    y = yacc_ref[...] + dsk_ref[...] * xs
    y = y * _silu(z_ref[...].astype(F32))
    for g in range(SSM_GROUPS):
        gs = slice(g * gw, (g + 1) * gw)
        yg = y[:, gs]
        ms = jnp.mean(yg * yg, axis=-1, keepdims=True)
        y_ref[:, gs] = (yg * lax.rsqrt(ms + EPS) * ng_ref[:, gs]).astype(y_ref.dtype)


def _ssm(proj3, small, smallT, conv_w, conv_b, dt_bias, a_log, d_skip, ssm_norm_g):
    bsz, s, _ = proj3.shape
    ln = SSM_CHUNK
    padl = lambda v: jnp.pad(v, (0, LANE - v.shape[0]))
    a = padl(-jnp.exp(a_log))
    dtb = padl(dt_bias)
    head_of = jnp.arange(SSM_D_INNER) // SSM_HEAD_DIM
    expand = (jnp.arange(LANE)[:, None] == head_of[None, :]).astype(jnp.bfloat16)
    full = lambda shp: pl.BlockSpec(shp, lambda b, c: (0,) * len(shp))
    return pl.pallas_call(
        _ssm_kernel,
        out_shape=jax.ShapeDtypeStruct((bsz, s, SSM_D_INNER), ACT_DTYPE),
        grid=(bsz, s // ln),
        in_specs=[pl.BlockSpec((None, ln, SSM_CONV_DIM), lambda b, c: (b, c, COL_XBC // SSM_CONV_DIM)),
                  pl.BlockSpec((None, ln, SSM_D_INNER), lambda b, c: (b, c, COL_ZSSM // SSM_D_INNER)),
                  pl.BlockSpec((None, ln, LANE), lambda b, c: (b, c, 0)),
                  pl.BlockSpec((None, LANE, ln), lambda b, c: (b, 0, c)),
                  full((SSM_CONV, SSM_CONV_DIM)), full((1, SSM_CONV_DIM)),
                  full((1, LANE)), full((LANE, 1)), full((1, LANE)), full((LANE, 1)),
                  full((1, SSM_D_INNER)), full((1, SSM_D_INNER)), full((LANE, SSM_D_INNER))],
        out_specs=pl.BlockSpec((None, ln, SSM_D_INNER), lambda b, c: (b, c, 0)),
        scratch_shapes=[pltpu.VMEM((SUBLANE + ln, SSM_CONV_DIM), F32),
                        pltpu.VMEM((SSM_STATE, SSM_D_INNER), F32),
                        pltpu.VMEM((ln, SSM_D_INNER), F32)],
        compiler_params=pltpu.CompilerParams(dimension_semantics=("parallel", "arbitrary"),
                                             vmem_limit_bytes=VMEM_LIMIT),
        name="ssm",
    )(proj3, proj3, small, smallT, conv_w, conv_b.reshape(1, -1),
      dtb.reshape(1, LANE), dtb.reshape(LANE, 1), a.reshape(1, LANE), a.reshape(LANE, 1),
      jnp.repeat(d_skip, SSM_HEAD_DIM).reshape(1, -1), ssm_norm_g.reshape(1, -1), expand)


def _final_kernel(o_ref, za_ref, y_ref, ga_ref, gs_ref, x_ref, gate_ref, wa_ref, ws_ref, wo_ref, out_ref):
    o = (o_ref[...].astype(F32) * _silu(za_ref[...].astype(F32))).astype(MXU_DTYPE)
    y_att = _dot(o, wa_ref[...])
    y_ssm = _dot(y_ref[...], ws_ref[...])
    merged = _sigmoid(ga_ref[...].astype(F32)) * y_att + _sigmoid(gs_ref[...].astype(F32)) * y_ssm
    out_ref[...] = x_ref[...] + gate_ref[...] * _dot(merged.astype(MXU_DTYPE), wo_ref[...])


def _final(o3, proj3, y3, x, gate, w_att, w_ssm, w_out, tm):
    bsz, s, d = x.shape
    tile = lambda w, c0: pl.BlockSpec((None, tm, w), lambda b, i: (b, i, c0 // w))
    full = lambda shp: pl.BlockSpec(shp, lambda b, i: (0,) * len(shp))
    return pl.pallas_call(
        _final_kernel,
        out_shape=jax.ShapeDtypeStruct((bsz, s, d), x.dtype),
        grid=(bsz, s // tm),
        in_specs=[tile(ATT_Q_DIM, 0), tile(ATT_Q_DIM, COL_ZATT), tile(SSM_D_INNER, 0),
                  tile(D_MODEL, COL_GATE), tile(D_MODEL, COL_GATE + D_MODEL), tile(d, 0),
                  pl.BlockSpec((None, 1, d), lambda b, i: (b, 0, 0)),
                  full((ATT_Q_DIM, d)), full((SSM_D_INNER, d)), full((d, d))],
        out_specs=tile(d, 0),
        compiler_params=pltpu.CompilerParams(dimension_semantics=("parallel", "parallel"),
                                             vmem_limit_bytes=VMEM_LIMIT),
        name="final",
    )(o3, proj3, y3, proj3, proj3, x, gate[:, None, :],
      w_att.astype(MXU_DTYPE), w_ssm.astype(MXU_DTYPE), w_out.astype(MXU_DTYPE))


def _permute_w_in(w_in):
    sizes = (ATT_Q_DIM, ATT_KV_DIM, ATT_KV_DIM, ATT_Q_DIM, IDX_Q_DIM, IDX_DIM, IDX_HEADS,
             SSM_D_INNER, SSM_CONV_DIM, SSM_HEADS, 2 * D_MODEL)
    offs = [0]
    for sz in sizes:
        offs.append(offs[-1] + sz)
    q, k, v, z_att, q_idx, k_idx, w_idx, z_ssm, xbc, dt, gate = (
        w_in[:, offs[n]:offs[n + 1]] for n in range(len(sizes)))
    d = w_in.shape[0]
    zeros = lambda n: jnp.zeros((d, n), w_in.dtype)
    wide = jnp.concatenate(
        [xbc, q, z_ssm, gate, z_att, k, v, q_idx, k_idx, zeros(LANE - IDX_DIM),
         zeros(WIDE_COLS - COL_KIDX - LANE)], axis=1)
    small = jnp.concatenate([dt, w_idx, zeros(LANE - SSM_HEADS - IDX_HEADS)], axis=1)
    return wide.astype(MXU_DTYPE), small


def _rope_tables(positions):
    def angles(rot):
        inv = jnp.power(ROPE_THETA, -(jnp.arange(0, rot, 2, dtype=F32) / rot))
        ang = positions.astype(F32)[..., None] * inv
        return jnp.cos(ang), jnp.sin(ang)

    def lanes(cos, sin, width):
        half = cos.shape[-1]
        lead = cos.shape[:-1]
        ones = jnp.ones(lead + (width - 2 * half,), F32)
        ct = jnp.concatenate([cos, cos, ones], axis=-1)
        st = jnp.concatenate([-sin, sin, 0.0 * ones], axis=-1)
        reps = LANE // width
        return (jnp.tile(ct, reps).reshape(-1, LANE), jnp.tile(st, reps).reshape(-1, LANE))

    ca, sa = lanes(*angles(ATT_ROT_DIM), ATT_HEAD_DIM)
    ci, si = lanes(*angles(IDX_ROT_DIM), IDX_DIM)
    return ca, sa, ci, si


def _tile(n, pref):
    return pref if n % pref == 0 else n


def kernel(x, c, positions, ada_w, ada_b, norm_g, w_in, q_norm_g, k_norm_g, idx_k_ln_g, idx_k_ln_b,
           conv_w, conv_b, dt_bias, a_log, d_skip, ssm_norm_g, w_branch_att, w_branch_ssm, w_out):
    bsz, s, d = x.shape
    t = bsz * s
    tabs = _rope_tables(positions)
    for l in range(ada_w.shape[0]):
        mod = _ada(c, ada_w[l], ada_b[l])
        shift, scale, gate = mod[:, :d], mod[:, d:2 * d], mod[:, 2 * d:]
        w_wide, w_small = _permute_w_in(w_in[l])
        h, small = _norm(x, norm_g[l], scale, shift, w_small, _tile(s, 512))
        proj = _proj(h.reshape(t, d), w_wide, _tile(t, 1024), 1024)
        proj3 = proj.reshape(bsz, s, WIDE_COLS)

        q, k, q_idx, k_idx = _prep(proj, tabs, q_norm_g[l], k_norm_g[l], idx_k_ln_g[l], idx_k_ln_b[l],
                                   _tile(t, 512))
        to_t = lambda a: jnp.swapaxes(a.reshape(bsz, s, -1), 1, 2)
        smallT = jnp.swapaxes(small, 1, 2)
        o3 = _attn(q.reshape(bsz, s, -1), to_t(q_idx), smallT, to_t(k), proj3, k_idx.reshape(bsz, s, LANE),
                   q_norm_g[l], k_norm_g[l], _tile(s, 256))

        y3 = _ssm(proj3, small, smallT, conv_w[l], conv_b[l], dt_bias[l], a_log[l], d_skip[l], ssm_norm_g[l])
        x = _final(o3, proj3, y3, x, gate, w_branch_att[l], w_branch_ssm[l], w_out[l], _tile(s, 512))
    return x
```

```python
import functools

import jax
import jax.numpy as jnp
from jax import lax
from jax.experimental import pallas as pl
from jax.experimental.pallas import tpu as pltpu

F32 = jnp.float32
MXU_DTYPE = jnp.bfloat16
ACT_DTYPE = jnp.bfloat16

D_MODEL = 1024
ATT_HEADS = 8
ATT_KV_HEADS = 2
ATT_HEAD_DIM = 128
ATT_GROUP = ATT_HEADS // ATT_KV_HEADS
ROPE_THETA = 500000.0
ATT_ROT_DIM = 32
IDX_HEADS = 4
IDX_DIM = 64
IDX_ROT_DIM = 16
TOPK_MAX = 256
SSM_D_INNER = 2048
SSM_HEAD_DIM = 64
SSM_HEADS = 32
SSM_GROUPS = 4
SSM_STATE = 128
SSM_CONV = 4
SSM_CHUNK = 128
SSM_CONV_DIM = SSM_D_INNER + 2 * SSM_GROUPS * SSM_STATE
EPS = 1e-6

ATT_Q_DIM = ATT_HEADS * ATT_HEAD_DIM
ATT_KV_DIM = ATT_KV_HEADS * ATT_HEAD_DIM
IDX_Q_DIM = IDX_HEADS * IDX_DIM
LANE = 128
SUBLANE = 8
INT_MIN = -(2 ** 31)
HALF_BIAS = 2 ** 15
NEG_BIG = -1e30
LOG2_E = 1.4426950408889634
SAFE_LOG2_LOGIT = 60.0
VMEM_LIMIT = 56 * 1024 * 1024

COL_XBC = 0
COL_Q = COL_XBC + SSM_CONV_DIM
COL_ZSSM = COL_Q + ATT_Q_DIM
COL_GATE = COL_ZSSM + SSM_D_INNER
COL_ZATT = COL_GATE + 2 * D_MODEL
COL_K = COL_ZATT + ATT_Q_DIM
COL_V = COL_K + ATT_KV_DIM
COL_QIDX = COL_V + ATT_KV_DIM
COL_KIDX = COL_QIDX + IDX_Q_DIM
WIDE_COLS = 10240
SMALL_DT = 0
SMALL_WIDX = SSM_HEADS


def _split3(x):
    hi = x.astype(jnp.bfloat16)
    r1 = x - hi.astype(F32)
    mid = r1.astype(jnp.bfloat16)
    lo = (r1 - mid.astype(F32)).astype(jnp.bfloat16)
    return hi, mid, lo


def _dot(a, b):
    return jnp.dot(a, b, preferred_element_type=F32)


def _dot_f32(a, b):
    a0, a1, a2 = _split3(a)
    b0, b1, b2 = _split3(b)
    return (_dot(a0, b0) + (_dot(a0, b1) + _dot(a1, b0))
            + (_dot(a0, b2) + _dot(a1, b1) + _dot(a2, b0)))


def _dot_x_sel(x, sel):
    m = x.shape[0]
    parts = jnp.concatenate(_split3(x), axis=0)
    r = _dot(parts, sel)
    return r[:m] + r[m:2 * m] + r[2 * m:]


def _dot_sel_x(sel, x):
    hi, mid, lo = _split3(x)
    return _dot(sel, hi) + _dot(sel, mid) + _dot(sel, lo)


def _sigmoid(x):
    return 1.0 / (1.0 + jnp.exp(-x))


def _silu(x):
    return x * _sigmoid(x)


def _softplus(x):
    return jnp.maximum(x, 0.0) + jnp.log(1.0 + jnp.exp(-jnp.abs(x)))


def _ada_kernel(c_ref, w_ref, b_ref, o_ref):
    o_ref[...] = _dot_f32(_silu(c_ref[...]), w_ref[...]) + b_ref[...]


def _ada(c, ada_w, ada_b):
    bsz = c.shape[0]
    return pl.pallas_call(
        _ada_kernel,
        out_shape=jax.ShapeDtypeStruct((bsz, 3 * D_MODEL), F32),
        grid=(3,),
        in_specs=[pl.BlockSpec((bsz, D_MODEL), lambda j: (0, 0)),
                  pl.BlockSpec((D_MODEL, D_MODEL), lambda j: (0, j)),
                  pl.BlockSpec((1, D_MODEL), lambda j: (0, j))],
        out_specs=pl.BlockSpec((bsz, D_MODEL), lambda j: (0, j)),
        compiler_params=pltpu.CompilerParams(dimension_semantics=("arbitrary",),
                                             vmem_limit_bytes=VMEM_LIMIT),
        name="ada",
    )(c, ada_w, ada_b.reshape(1, -1))


def _norm_kernel(x_ref, g_ref, scale_ref, shift_ref, wsm_ref, h_ref, small_ref):
    x = x_ref[...]
    ms = jnp.mean(x * x, axis=-1, keepdims=True)
    y = x * lax.rsqrt(ms + EPS) * g_ref[...]
    h = y * (1.0 + scale_ref[...]) + shift_ref[...]
    h_ref[...] = h.astype(h_ref.dtype)
    small_ref[...] = _dot_f32(h, wsm_ref[...])


def _norm(x, norm_g, scale, shift, w_small, tm):
    bsz, s, d = x.shape
    return pl.pallas_call(
        _norm_kernel,
        out_shape=(jax.ShapeDtypeStruct((bsz, s, d), ACT_DTYPE),
                   jax.ShapeDtypeStruct((bsz, s, LANE), F32)),
        grid=(bsz, s // tm),
        in_specs=[pl.BlockSpec((None, tm, d), lambda b, i: (b, i, 0)),
                  pl.BlockSpec((1, d), lambda b, i: (0, 0)),
                  pl.BlockSpec((None, 1, d), lambda b, i: (b, 0, 0)),
                  pl.BlockSpec((None, 1, d), lambda b, i: (b, 0, 0)),
                  pl.BlockSpec((d, LANE), lambda b, i: (0, 0))],
        out_specs=(pl.BlockSpec((None, tm, d), lambda b, i: (b, i, 0)),
                   pl.BlockSpec((None, tm, LANE), lambda b, i: (b, i, 0))),
        compiler_params=pltpu.CompilerParams(dimension_semantics=("parallel", "parallel"),
                                             vmem_limit_bytes=VMEM_LIMIT),
        name="norm",
    )(x, norm_g.reshape(1, d), scale[:, None, :], shift[:, None, :], w_small)


def _proj_kernel(h_ref, w_ref, o_ref):
    o_ref[...] = _dot(h_ref[...], w_ref[...]).astype(o_ref.dtype)


def _proj(h2d, w_wide, tm, tn):
    t, d = h2d.shape
    n = w_wide.shape[1]
    return pl.pallas_call(
        _proj_kernel,
        out_shape=jax.ShapeDtypeStruct((t, n), ACT_DTYPE),
        grid=(t // tm, n // tn),
        in_specs=[pl.BlockSpec((tm, d), lambda i, j: (i, 0)),
                  pl.BlockSpec((d, tn), lambda i, j: (0, j))],
        out_specs=pl.BlockSpec((tm, tn), lambda i, j: (i, j)),
        compiler_params=pltpu.CompilerParams(dimension_semantics=("parallel", "arbitrary"),
                                             vmem_limit_bytes=VMEM_LIMIT),
        name="proj",
    )(h2d, w_wide)


def _rope(y, cos_t, sin_t, first_half, half):
    fwd = pltpu.roll(y, LANE - half, 1)
    bwd = pltpu.roll(y, half, 1)
    return y * cos_t + jnp.where(first_half, fwd, bwd) * sin_t


def _prep_kernel(q_ref, k_ref, qi_ref, ki_ref, ca_ref, sa_ref, ci_ref, si_ref,
                 qg_ref, kg_ref, lg_ref, lb_ref, qo_ref, ko_ref, qio_ref, kio_ref):
    tm = q_ref.shape[0]
    lane = lax.broadcasted_iota(jnp.int32, (tm, LANE), 1)
    ca, sa = ca_ref[...], sa_ref[...]
    ci, si = ci_ref[...], si_ref[...]
    att_first = lane < ATT_ROT_DIM // 2
    idx_first = (lane % IDX_DIM) < IDX_ROT_DIM // 2

    def norm_rope(x, g):
        ms = jnp.mean(x * x, axis=-1, keepdims=True)
        y = x * lax.rsqrt(ms + EPS) * g
        return _rope(y, ca, sa, att_first, ATT_ROT_DIM // 2)

    q_scale = ATT_HEAD_DIM ** -0.5 * LOG2_E
    for h in range(ATT_HEADS):
        sl = slice(h * LANE, (h + 1) * LANE)
        qo_ref[:, sl] = (norm_rope(q_ref[:, sl].astype(F32), qg_ref[...]) * q_scale).astype(qo_ref.dtype)
    for h in range(ATT_KV_HEADS):
        sl = slice(h * LANE, (h + 1) * LANE)
        ko_ref[:, sl] = norm_rope(k_ref[:, sl].astype(F32), kg_ref[...]).astype(ko_ref.dtype)
    for j in range(IDX_Q_DIM // LANE):
        sl = slice(j * LANE, (j + 1) * LANE)
        y = _rope(qi_ref[:, sl].astype(F32), ci, si, idx_first, IDX_ROT_DIM // 2)
        qio_ref[:, sl] = (y * IDX_DIM ** -0.5).astype(qio_ref.dtype)
    x = ki_ref[...].astype(F32)
    real = lane < IDX_DIM
    mu = jnp.sum(jnp.where(real, x, 0.0), axis=-1, keepdims=True) * (1.0 / IDX_DIM)
    dlt = jnp.where(real, x - mu, 0.0)
    var = jnp.sum(dlt * dlt, axis=-1, keepdims=True) * (1.0 / IDX_DIM)
    y = dlt * lax.rsqrt(var + EPS) * lg_ref[...] + lb_ref[...]
    kio_ref[...] = _rope(y, ci, si, idx_first, IDX_ROT_DIM // 2).astype(kio_ref.dtype)


def _prep(proj, tabs, q_norm_g, k_norm_g, ln_g, ln_b, tm):
    t = proj.shape[0]
    ca, sa, ci, si = tabs
    row = lambda w, c0: pl.BlockSpec((tm, w), lambda i: (i, c0 // w))
    tab = pl.BlockSpec((tm, LANE), lambda i: (i, 0))
    par = pl.BlockSpec((1, LANE), lambda i: (0, 0))
    pad = lambda v: jnp.pad(v, (0, LANE - v.shape[0])).reshape(1, LANE)
    out = lambda w: pl.BlockSpec((tm, w), lambda i: (i, 0))
    return pl.pallas_call(
        _prep_kernel,
        out_shape=(jax.ShapeDtypeStruct((t, ATT_Q_DIM), MXU_DTYPE),
                   jax.ShapeDtypeStruct((t, ATT_KV_DIM), MXU_DTYPE),
                   jax.ShapeDtypeStruct((t, IDX_Q_DIM), MXU_DTYPE),
                   jax.ShapeDtypeStruct((t, LANE), MXU_DTYPE)),
        grid=(t // tm,),
        in_specs=[row(ATT_Q_DIM, COL_Q), row(ATT_KV_DIM, COL_K), row(IDX_Q_DIM, COL_QIDX), row(LANE, COL_KIDX),
                  tab, tab, tab, tab, par, par, par, par],
        out_specs=(out(ATT_Q_DIM), out(ATT_KV_DIM), out(IDX_Q_DIM), out(LANE)),
        compiler_params=pltpu.CompilerParams(dimension_semantics=("parallel",),
                                             vmem_limit_bytes=VMEM_LIMIT),
        name="prep",
    )(proj, proj, proj, proj, ca, sa, ci, si,
      q_norm_g.reshape(1, LANE), k_norm_g.reshape(1, LANE), pad(ln_g), pad(ln_b))


def _attn_kernel(q_ref, qiT_ref, smT_ref, kT_ref, v_ref, ki_ref, qg_ref, kg_ref, o_ref,
                 keys_ref, hi_ref, lo_ref, m_ref, acc_ref, qcat_ref, *, tq, topk):
    i = pl.program_id(1)
    nchunk = i + 1
    tk = tq
    w_idx = smT_ref[SMALL_WIDX:SMALL_WIDX + IDX_HEADS, :] * (IDX_HEADS ** -0.5)
    qi_all = jnp.concatenate([qiT_ref[h * IDX_DIM:(h + 1) * IDX_DIM, :] for h in range(IDX_HEADS)], axis=1)
    row_iota = lax.broadcasted_iota(jnp.int32, (tk, tq), 0)
    col_iota = lax.broadcasted_iota(jnp.int32, (tk, tq), 1)

    def score_chunk(c, diagonal):
        r0 = pl.multiple_of(c * tk, tk)
        kc = ki_ref[pl.ds(r0, tk), :][:, :IDX_DIM]
        lg = _dot(kc, qi_all)
        s = w_idx[0:1, :] * jnp.maximum(lg[:, :tq], 0.0)
        for h in range(1, IDX_HEADS):
            s = s + w_idx[h:h + 1, :] * jnp.maximum(lg[:, h * tq:(h + 1) * tq], 0.0)
        bits = lax.bitcast_convert_type(s + 0.0, jnp.int32)
        key = jnp.where(bits < 0, bits ^ jnp.int32(0x7FFFFFFF), bits)
        if diagonal:
            key = jnp.where(row_iota <= col_iota, key, jnp.int32(INT_MIN))
        keys_ref[pl.ds(r0, tk), :] = key
        hi_ref[pl.ds(r0, tk), :] = (key >> 16).astype(jnp.int16)
        lo_ref[pl.ds(r0, tk), :] = ((key & 0xFFFF) - HALF_BIAS).astype(jnp.int16)

    def score_body(c, carry):
        score_chunk(c, False)
        return carry

    lax.fori_loop(0, nchunk - 1, score_body, 0)
    score_chunk(nchunk - 1, True)

    def count(pred):
        def body(c, acc):
            kk = keys_ref[pl.ds(pl.multiple_of(c * tk, tk), tk), :]
            hit = jnp.where(pred(kk), jnp.int32(1), jnp.int32(0))
            return acc + hit.reshape(tk // SUBLANE, SUBLANE, tq).sum(axis=0)
        acc = lax.fori_loop(0, nchunk, body, jnp.zeros((SUBLANE, tq), jnp.int32))
        return acc.sum(axis=0, keepdims=True)

    pack = 2 * SUBLANE

    def count16(ref, pred):
        def body(c, acc):
            kk = ref[pl.ds(pl.multiple_of(c * tk, tk), tk), :]
            hit = jnp.where(pred(kk), jnp.bfloat16(1), jnp.bfloat16(0))
            parts = [hit[r * pack:(r + 1) * pack, :] for r in range(tk // pack)]
            while len(parts) > 1:
                parts = [parts[n] + parts[n + 1] for n in range(0, len(parts), 2)]
            return acc + parts[0].astype(F32)
        acc = lax.fori_loop(0, nchunk, body, jnp.zeros((pack, tq), F32))
        return acc.sum(axis=0, keepdims=True)

    def bisect16(ref, want):
        def bit_body(j, ans):
            trial = ans + lax.shift_left(jnp.int32(1), 15 - j)
            t16 = trial.astype(jnp.int16)
            cnt = count16(ref, lambda kk: kk >= t16)
            return jnp.where(cnt >= want, trial, ans)
        return lax.fori_loop(0, 16, bit_body, jnp.full((1, tq), -HALF_BIAS, jnp.int32))

    ans_hi = bisect16(hi_ref, float(topk))
    hi16 = ans_hi.astype(jnp.int16)
    want_lo = float(topk) - count16(hi_ref, lambda kk: kk > hi16)

    def restrict_body(c, carry):
        sl = pl.ds(pl.multiple_of(c * tk, tk), tk)
        lo_ref[sl, :] = jnp.where(hi_ref[sl, :] == hi16, lo_ref[sl, :], jnp.int16(-HALF_BIAS))
        return carry

    lax.fori_loop(0, nchunk, restrict_body, 0)
    ans_lo = bisect16(lo_ref, want_lo)
    ans = lax.shift_left(ans_hi, 16) | (ans_lo + HALF_BIAS)
    thr = jnp.maximum(ans, jnp.int32(INT_MIN + 1))
    cnt_gt = count(lambda kk: kk > thr)
    cnt_eq = count(lambda kk: kk == thr)
    need = topk - cnt_gt
    any_surplus_tie = jnp.max(jnp.where(cnt_eq > need, 1, 0)) > 0

    for g in range(ATT_KV_HEADS):
        qcat_ref[g] = jnp.concatenate(
            [q_ref[:, h * LANE:(h + 1) * LANE] for h in range(g * ATT_GROUP, (g + 1) * ATT_GROUP)], axis=0)
    acc_ref[...] = jnp.zeros(acc_ref.shape, F32)
    ones = jnp.ones((tk, LANE), MXU_DTYPE)

    bound = (ATT_HEAD_DIM ** 0.5 * LOG2_E) * jnp.max(jnp.abs(qg_ref[...])) * jnp.max(jnp.abs(kg_ref[...]))
    bounded = bound <= SAFE_LOG2_LOGIT

    def threshold_bias(c, taken):
        kk = keys_ref[pl.ds(pl.multiple_of(c * tk, tk), tk), :]
        return jnp.where(kk >= thr, 0.0, NEG_BIG).T, taken

    need_f = need.astype(F32)
    before = jnp.where(col_iota < row_iota, 1.0, 0.0).astype(jnp.bfloat16)

    def tie_bias(c, taken):
        kk = keys_ref[pl.ds(pl.multiple_of(c * tk, tk), tk), :]
        eq = kk == thr
        eq_b = jnp.where(eq, 1.0, 0.0).astype(jnp.bfloat16)
        rank = _dot(before, eq_b) + taken
        tie = jnp.where(eq, jnp.where(rank < need_f, 0.0, NEG_BIG), NEG_BIG)
        bias = jnp.where(kk > thr, 0.0, tie)
        return bias.T, taken + jnp.sum(eq_b.astype(F32), axis=0, keepdims=True)

    def masked_logits(c, g, bias_t):
        r0 = pl.multiple_of(c * tk, tk)
        s = _dot(qcat_ref[g], kT_ref[g * LANE:(g + 1) * LANE, pl.ds(r0, tk)])
        vx = jnp.concatenate([v_ref[pl.ds(r0, tk), g * LANE:(g + 1) * LANE], ones], axis=1)
        return [s[r * tq:(r + 1) * tq, :] + bias_t for r in range(ATT_GROUP)], vx

    def plain_softmax(c, bias_t):
        for g in range(ATT_KV_HEADS):
            s, vx = masked_logits(c, g, bias_t)
            p = jnp.concatenate([jnp.exp2(sr).astype(MXU_DTYPE) for sr in s], axis=0)
            acc_ref[g] += _dot(p, vx)

    def online_softmax(c, bias_t):
        for g in range(ATT_KV_HEADS):
            s, vx = masked_logits(c, g, bias_t)
            s = jnp.concatenate(s, axis=0)
            m_old = m_ref[g]
            m_new = jnp.maximum(m_old, jnp.max(s, axis=1, keepdims=True))
            p = jnp.exp2(s - m_new)
            acc_ref[g] = jnp.exp2(m_old - m_new) * acc_ref[g] + _dot(p.astype(MXU_DTYPE), vx)
            m_ref[g] = m_new

    def run(bias_fn, softmax_fn):
        def body(c, taken):
            bias_t, taken = bias_fn(c, taken)
            softmax_fn(c, bias_t)
            return taken
        lax.fori_loop(0, nchunk, body, jnp.zeros((1, tq), F32))

    @pl.when(jnp.logical_and(bounded, jnp.logical_not(any_surplus_tie)))
    def _():
        run(threshold_bias, plain_softmax)

    @pl.when(jnp.logical_and(bounded, any_surplus_tie))
    def _():
        run(tie_bias, plain_softmax)

    @pl.when(jnp.logical_not(bounded))
    def _():
        m_ref[...] = jnp.full(m_ref.shape, NEG_BIG, F32)
        run(tie_bias, online_softmax)

    for h in range(ATT_HEADS):
        g, rs = h // ATT_GROUP, slice((h % ATT_GROUP) * tq, (h % ATT_GROUP + 1) * tq)
        o_ref[:, h * LANE:(h + 1) * LANE] = (acc_ref[g, rs, :LANE] / acc_ref[g, rs, LANE:]).astype(o_ref.dtype)


def _attn(q, qiT, smallT, kT, proj3, ki, q_norm_g, k_norm_g, tq):
    bsz, s, _ = q.shape
    topk = min(TOPK_MAX, s // 4)
    kern = functools.partial(_attn_kernel, tq=tq, topk=topk)
    qtile_t = lambda r: pl.BlockSpec((None, r, tq), lambda b, i: (b, 0, i))
    par = pl.BlockSpec((1, LANE), lambda b, i: (0, 0))
    return pl.pallas_call(
        kern,
        out_shape=jax.ShapeDtypeStruct((bsz, s, ATT_Q_DIM), ACT_DTYPE),
        grid=(bsz, s // tq),
        in_specs=[pl.BlockSpec((None, tq, ATT_Q_DIM), lambda b, i: (b, i, 0)),
                  qtile_t(IDX_Q_DIM), qtile_t(LANE),
                  pl.BlockSpec((None, ATT_KV_DIM, s), lambda b, i: (b, 0, 0)),
                  pl.BlockSpec((None, s, ATT_KV_DIM), lambda b, i: (b, 0, COL_V // ATT_KV_DIM)),
                  pl.BlockSpec((None, s, LANE), lambda b, i: (b, 0, 0)), par, par],
        out_specs=pl.BlockSpec((None, tq, ATT_Q_DIM), lambda b, i: (b, i, 0)),
        scratch_shapes=[pltpu.VMEM((s, tq), jnp.int32),
                        pltpu.VMEM((s, tq), jnp.int16),
                        pltpu.VMEM((s, tq), jnp.int16),
                        pltpu.VMEM((ATT_KV_HEADS, ATT_GROUP * tq, 1), F32),
                        pltpu.VMEM((ATT_KV_HEADS, ATT_GROUP * tq, 2 * LANE), F32),
                        pltpu.VMEM((ATT_KV_HEADS, ATT_GROUP * tq, ATT_HEAD_DIM), MXU_DTYPE)],
        compiler_params=pltpu.CompilerParams(dimension_semantics=("parallel", "arbitrary"),
                                             vmem_limit_bytes=VMEM_LIMIT),
        name="attn",
    )(q, qiT, smallT, kT, proj3, ki, q_norm_g.reshape(1, LANE), k_norm_g.reshape(1, LANE))


def _ssm_kernel(xbc_ref, z_ref, sm_ref, smT_ref, cw_ref, cb_ref, dtb_ref, dtbT_ref, a_ref, aT_ref,
                dsk_ref, ng_ref, exp_ref, y_ref, ext_ref, st_ref, yacc_ref):
    ln = SSM_CHUNK
    c = pl.program_id(1)

    @pl.when(c == 0)
    def _():
        ext_ref[0:SUBLANE, :] = jnp.zeros((SUBLANE, SSM_CONV_DIM), F32)
        st_ref[...] = jnp.zeros(st_ref.shape, F32)

    raw = xbc_ref[...].astype(F32)
    ext_ref[SUBLANE:SUBLANE + ln, :] = raw
    acc = cb_ref[...] + cw_ref[0:1, :] * ext_ref[SUBLANE - 3:SUBLANE - 3 + ln, :]
    for t in range(1, SSM_CONV):
        acc = acc + cw_ref[t:t + 1, :] * ext_ref[SUBLANE - 3 + t:SUBLANE - 3 + t + ln, :]
    ext_ref[0:SUBLANE, :] = raw[ln - SUBLANE:, :]
    xbc = _silu(acc)
    xs = xbc[:, :SSM_D_INNER]
    b_all = xbc[:, SSM_D_INNER:SSM_D_INNER + SSM_GROUPS * SSM_STATE].astype(MXU_DTYPE)
    c_all = xbc[:, SSM_D_INNER + SSM_GROUPS * SSM_STATE:].astype(MXU_DTYPE)

    dt = _softplus(sm_ref[...] + dtb_ref[...])
    adt = dt * a_ref[...]
    r_i = lax.broadcasted_iota(jnp.int32, (ln, ln), 0)
    c_i = lax.broadcasted_iota(jnp.int32, (ln, ln), 1)
    tril = r_i >= c_i
    incl = jnp.where(tril, 1.0, 0.0).astype(jnp.bfloat16)
    a_col = _dot_sel_x(incl, adt)
    dtT = _softplus(smT_ref[...] + dtbT_ref[...])
    incl_t = jnp.where(r_i <= c_i, 1.0, 0.0).astype(jnp.bfloat16)
    a_row = _dot_x_sel(dtT * aT_ref[...], incl_t)

    expand = exp_ref[...]
    wide = _dot_x_sel(jnp.concatenate([a_col, dt], axis=0), expand)
    a_wide, dt_wide = wide[:ln], wide[ln:]
    a_last = a_wide[ln - 1:ln, :]
    xc = xs * dt_wide
    xc_b = xc.astype(MXU_DTYPE)
    xd_b = (xc * jnp.exp(a_last - a_wide)).astype(MXU_DTYPE)
    e_wide = jnp.exp(a_wide)
    chunk_decay = e_wide[ln - 1:ln, :]

    lane = lax.broadcasted_iota(jnp.int32, (ln, LANE), 1)
    low = lane < SSM_HEAD_DIM
    gw = SSM_D_INNER // SSM_GROUPS
    for g in range(SSM_GROUPS):
        gs = slice(g * gw, (g + 1) * gw)
        b_g = b_all[:, g * SSM_STATE:(g + 1) * SSM_STATE]
        c_g = c_all[:, g * SSM_STATE:(g + 1) * SSM_STATE]
        cb = lax.dot_general(c_g, b_g, (((1,), (1,)), ((), ())), preferred_element_type=F32)
        y_off = _dot(c_g, st_ref[:, gs].astype(MXU_DTYPE)) * e_wide[:, gs]
        st_ref[:, gs] = chunk_decay[:, gs] * st_ref[:, gs] + lax.dot_general(
            b_g, xd_b[:, gs], (((0,), (0,)), ((), ())), preferred_element_type=F32)
        yacc_ref[:, gs] = y_off
        for jp in range(gw // LANE):
            j = g * (gw // LANE) + jp
            xp = xc_b[:, j * LANE:(j + 1) * LANE]
            yd = None
            for half, keep in ((0, low), (1, jnp.logical_not(low))):
                hd = 2 * j + half
                diff = a_col[:, hd:hd + 1] - a_row[hd:hd + 1, :]
                lmat = jnp.exp(jnp.where(tril, diff, -jnp.inf))
                mh = (cb * lmat).astype(MXU_DTYPE)
                part = _dot(mh, jnp.where(keep, xp, jnp.zeros_like(xp)))
                yd = part if yd is None else yd + part
            yacc_ref[:, j * LANE:(j + 1) * LANE] += yd

    y = yacc_ref[...] + dsk_ref[...] * xs
    y = y * _silu(z_ref[...].astype(F32))
    for g in range(SSM_GROUPS):
        gs = slice(g * gw, (g + 1) * gw)
        yg = y[:, gs]
        ms = jnp.mean(yg * yg, axis=-1, keepdims=True)
        y_ref[:, gs] = (yg * lax.rsqrt(ms + EPS) * ng_ref[:, gs]).astype(y_ref.dtype)


def _ssm(proj3, small, smallT, conv_w, conv_b, dt_bias, a_log, d_skip, ssm_norm_g):
    bsz, s, _ = proj3.shape
    ln = SSM_CHUNK
    padl = lambda v: jnp.pad(v, (0, LANE - v.shape[0]))
    a = padl(-jnp.exp(a_log))
    dtb = padl(dt_bias)
    head_of = jnp.arange(SSM_D_INNER) // SSM_HEAD_DIM
    expand = (jnp.arange(LANE)[:, None] == head_of[None, :]).astype(jnp.bfloat16)
    full = lambda shp: pl.BlockSpec(shp, lambda b, c: (0,) * len(shp))
    return pl.pallas_call(
        _ssm_kernel,
        out_shape=jax.ShapeDtypeStruct((bsz, s, SSM_D_INNER), ACT_DTYPE),
        grid=(bsz, s // ln),
        in_specs=[pl.BlockSpec((None, ln, SSM_CONV_DIM), lambda b, c: (b, c, COL_XBC // SSM_CONV_DIM)),
                  pl.BlockSpec((None, ln, SSM_D_INNER), lambda b, c: (b, c, COL_ZSSM // SSM_D_INNER)),
                  pl.BlockSpec((None, ln, LANE), lambda b, c: (b, c, 0)),
                  pl.BlockSpec((None, LANE, ln), lambda b, c: (b, 0, c)),
                  full((SSM_CONV, SSM_CONV_DIM)), full((1, SSM_CONV_DIM)),
                  full((1, LANE)), full((LANE, 1)), full((1, LANE)), full((LANE, 1)),
                  full((1, SSM_D_INNER)), full((1, SSM_D_INNER)), full((LANE, SSM_D_INNER))],
        out_specs=pl.BlockSpec((None, ln, SSM_D_INNER), lambda b, c: (b, c, 0)),
        scratch_shapes=[pltpu.VMEM((SUBLANE + ln, SSM_CONV_DIM), F32),
                        pltpu.VMEM((SSM_STATE, SSM_D_INNER), F32),
                        pltpu.VMEM((ln, SSM_D_INNER), F32)],
        compiler_params=pltpu.CompilerParams(dimension_semantics=("parallel", "arbitrary"),
                                             vmem_limit_bytes=VMEM_LIMIT),
        name="ssm",
    )(proj3, proj3, small, smallT, conv_w, conv_b.reshape(1, -1),
      dtb.reshape(1, LANE), dtb.reshape(LANE, 1), a.reshape(1, LANE), a.reshape(LANE, 1),
      jnp.repeat(d_skip, SSM_HEAD_DIM).reshape(1, -1), ssm_norm_g.reshape(1, -1), expand)


def _final_kernel(o_ref, za_ref, y_ref, ga_ref, gs_ref, x_ref, gate_ref, wa_ref, ws_ref, wo_ref, out_ref):
    o = (o_ref[...].astype(F32) * _silu(za_ref[...].astype(F32))).astype(MXU_DTYPE)
    y_att = _dot(o, wa_ref[...])
    y_ssm = _dot(y_ref[...], ws_ref[...])
    merged = _sigmoid(ga_ref[...].astype(F32)) * y_att + _sigmoid(gs_ref[...].astype(F32)) * y_ssm
    out_ref[...] = x_ref[...] + gate_ref[...] * _dot(merged.astype(MXU_DTYPE), wo_ref[...])


def _final(o3, proj3, y3, x, gate, w_att, w_ssm, w_out, tm):
    bsz, s, d = x.shape
    tile = lambda w, c0: pl.BlockSpec((None, tm, w), lambda b, i: (b, i, c0 // w))
    full = lambda shp: pl.BlockSpec(shp, lambda b, i: (0,) * len(shp))
    return pl.pallas_call(
        _final_kernel,
        out_shape=jax.ShapeDtypeStruct((bsz, s, d), x.dtype),
        grid=(bsz, s // tm),
        in_specs=[tile(ATT_Q_DIM, 0), tile(ATT_Q_DIM, COL_ZATT), tile(SSM_D_INNER, 0),
                  tile(D_MODEL, COL_GATE), tile(D_MODEL, COL_GATE + D_MODEL), tile(d, 0),
                  pl.BlockSpec((None, 1, d), lambda b, i: (b, 0, 0)),
                  full((ATT_Q_DIM, d)), full((SSM_D_INNER, d)), full((d, d))],
        out_specs=tile(d, 0),
        compiler_params=pltpu.CompilerParams(dimension_semantics=("parallel", "parallel"),
                                             vmem_limit_bytes=VMEM_LIMIT),
        name="final",
    )(o3, proj3, y3, proj3, proj3, x, gate[:, None, :],
      w_att.astype(MXU_DTYPE), w_ssm.astype(MXU_DTYPE), w_out.astype(MXU_DTYPE))


def _permute_w_in(w_in):
    sizes = (ATT_Q_DIM, ATT_KV_DIM, ATT_KV_DIM, ATT_Q_DIM, IDX_Q_DIM, IDX_DIM, IDX_HEADS,
             SSM_D_INNER, SSM_CONV_DIM, SSM_HEADS, 2 * D_MODEL)
    offs = [0]
    for sz in sizes:
        offs.append(offs[-1] + sz)
    q, k, v, z_att, q_idx, k_idx, w_idx, z_ssm, xbc, dt, gate = (
        w_in[:, offs[n]:offs[n + 1]] for n in range(len(sizes)))
    d = w_in.shape[0]
    zeros = lambda n: jnp.zeros((d, n), w_in.dtype)
    wide = jnp.concatenate(
        [xbc, q, z_ssm, gate, z_att, k, v, q_idx, k_idx, zeros(LANE - IDX_DIM),
         zeros(WIDE_COLS - COL_KIDX - LANE)], axis=1)
    small = jnp.concatenate([dt, w_idx, zeros(LANE - SSM_HEADS - IDX_HEADS)], axis=1)
    return wide.astype(MXU_DTYPE), small


def _rope_tables(positions):
    def angles(rot):
        inv = jnp.power(ROPE_THETA, -(jnp.arange(0, rot, 2, dtype=F32) / rot))
        ang = positions.astype(F32)[..., None] * inv
        return jnp.cos(ang), jnp.sin(ang)

    def lanes(cos, sin, width):
        half = cos.shape[-1]
        lead = cos.shape[:-1]
        ones = jnp.ones(lead + (width - 2 * half,), F32)
        ct = jnp.concatenate([cos, cos, ones], axis=-1)
        st = jnp.concatenate([-sin, sin, 0.0 * ones], axis=-1)
        reps = LANE // width
        return (jnp.tile(ct, reps).reshape(-1, LANE), jnp.tile(st, reps).reshape(-1, LANE))

    ca, sa = lanes(*angles(ATT_ROT_DIM), ATT_HEAD_DIM)
    ci, si = lanes(*angles(IDX_ROT_DIM), IDX_DIM)
    return ca, sa, ci, si


def _tile(n, pref):
    return pref if n % pref == 0 else n


def kernel(x, c, positions, ada_w, ada_b, norm_g, w_in, q_norm_g, k_norm_g, idx_k_ln_g, idx_k_ln_b,
           conv_w, conv_b, dt_bias, a_log, d_skip, ssm_norm_g, w_branch_att, w_branch_ssm, w_out):
    bsz, s, d = x.shape
    t = bsz * s
    tabs = _rope_tables(positions)
    for l in range(ada_w.shape[0]):
        mod = _ada(c, ada_w[l], ada_b[l])
        shift, scale, gate = mod[:, :d], mod[:, d:2 * d], mod[:, 2 * d:]
        w_wide, w_small = _permute_w_in(w_in[l])
        h, small = _norm(x, norm_g[l], scale, shift, w_small, _tile(s, 512))
        proj = _proj(h.reshape(t, d), w_wide, _tile(t, 1024), 1024)
        proj3 = proj.reshape(bsz, s, WIDE_COLS)

        q, k, q_idx, k_idx = _prep(proj, tabs, q_norm_g[l], k_norm_g[l], idx_k_ln_g[l], idx_k_ln_b[l],
                                   _tile(t, 512))
        to_t = lambda a: jnp.swapaxes(a.reshape(bsz, s, -1), 1, 2)
        smallT = jnp.swapaxes(small, 1, 2)
        o3 = _attn(q.reshape(bsz, s, -1), to_t(q_idx), smallT, to_t(k), proj3, k_idx.reshape(bsz, s, LANE),
                   q_norm_g[l], k_norm_g[l], _tile(s, 256))

        y3 = _ssm(proj3, small, smallT, conv_w[l], conv_b[l], dt_bias[l], a_log[l], d_skip[l], ssm_norm_g[l])
        x = _final(o3, proj3, y3, x, gate, w_branch_att[l], w_branch_ssm[l], w_out[l], _tile(s, 512))
    return x
```

```python
import functools

import jax
import jax.numpy as jnp
from jax import lax
from jax.experimental import pallas as pl
from jax.experimental.pallas import tpu as pltpu

F32 = jnp.float32
MXU_DTYPE = jnp.bfloat16
ACT_DTYPE = jnp.bfloat16

D_MODEL = 1024
ATT_HEADS = 8
ATT_KV_HEADS = 2
ATT_HEAD_DIM = 128
ATT_GROUP = ATT_HEADS // ATT_KV_HEADS
ROPE_THETA = 500000.0
ATT_ROT_DIM = 32
IDX_HEADS = 4
IDX_DIM = 64
IDX_ROT_DIM = 16
TOPK_MAX = 256
SSM_D_INNER = 2048
SSM_HEAD_DIM = 64
SSM_HEADS = 32
SSM_GROUPS = 4
SSM_STATE = 128
SSM_CONV = 4
SSM_CHUNK = 128
SSM_CONV_DIM = SSM_D_INNER + 2 * SSM_GROUPS * SSM_STATE
EPS = 1e-6

ATT_Q_DIM = ATT_HEADS * ATT_HEAD_DIM
ATT_KV_DIM = ATT_KV_HEADS * ATT_HEAD_DIM
IDX_Q_DIM = IDX_HEADS * IDX_DIM
LANE = 128
SUBLANE = 8
INT_MIN = -(2 ** 31)
HALF_BIAS = 2 ** 15
NEG_BIG = -1e30
LOG2_E = 1.4426950408889634
SAFE_LOG2_LOGIT = 60.0
VMEM_LIMIT = 56 * 1024 * 1024

COL_ZSSM = 0
COL_GATE = COL_ZSSM + SSM_D_INNER
COL_ZATT = COL_GATE + 2 * D_MODEL
COL_Q = COL_ZATT + ATT_Q_DIM
COL_K = COL_Q + ATT_Q_DIM
COL_V = COL_K + ATT_KV_DIM
COL_QIDX = COL_V + ATT_KV_DIM
COL_KIDX = COL_QIDX + IDX_Q_DIM
WIDE_COLS = 7168
SMALL_DT = 0
SMALL_WIDX = SSM_HEADS


def _split3(x):
    hi = x.astype(jnp.bfloat16)
    r1 = x - hi.astype(F32)
    mid = r1.astype(jnp.bfloat16)
    lo = (r1 - mid.astype(F32)).astype(jnp.bfloat16)
    return hi, mid, lo


def _dot(a, b):
    return jnp.dot(a, b, preferred_element_type=F32)


def _dot_f32(a, b):
    a0, a1, a2 = _split3(a)
    b0, b1, b2 = _split3(b)
    return (_dot(a0, b0) + (_dot(a0, b1) + _dot(a1, b0))
            + (_dot(a0, b2) + _dot(a1, b1) + _dot(a2, b0)))


def _dot_x_sel(x, sel):
    m = x.shape[0]
    parts = jnp.concatenate(_split3(x), axis=0)
    r = _dot(parts, sel)
    return r[:m] + r[m:2 * m] + r[2 * m:]


def _dot_sel_x(sel, x):
    hi, mid, lo = _split3(x)
    return _dot(sel, hi) + _dot(sel, mid) + _dot(sel, lo)


def _sigmoid(x):
    return 1.0 / (1.0 + jnp.exp(-x))


def _silu(x):
    return x * _sigmoid(x)


def _softplus(x):
    return jnp.maximum(x, 0.0) + jnp.log(1.0 + jnp.exp(-jnp.abs(x)))


def _ada_kernel(c_ref, w_ref, b_ref, o_ref):
    o_ref[...] = _dot_f32(_silu(c_ref[...]), w_ref[...]) + b_ref[...]


def _ada(c, ada_w, ada_b):
    bsz = c.shape[0]
    return pl.pallas_call(
        _ada_kernel,
        out_shape=jax.ShapeDtypeStruct((bsz, 3 * D_MODEL), F32),
        grid=(3,),
        in_specs=[pl.BlockSpec((bsz, D_MODEL), lambda j: (0, 0)),
                  pl.BlockSpec((D_MODEL, D_MODEL), lambda j: (0, j)),
                  pl.BlockSpec((1, D_MODEL), lambda j: (0, j))],
        out_specs=pl.BlockSpec((bsz, D_MODEL), lambda j: (0, j)),
        compiler_params=pltpu.CompilerParams(dimension_semantics=("arbitrary",),
                                             vmem_limit_bytes=VMEM_LIMIT),
        name="ada",
    )(c, ada_w, ada_b.reshape(1, -1))


def _norm_kernel(x_ref, g_ref, scale_ref, shift_ref, wsm_ref, h_ref, small_ref):
    x = x_ref[...]
    ms = jnp.mean(x * x, axis=-1, keepdims=True)
    y = x * lax.rsqrt(ms + EPS) * g_ref[...]
    h = y * (1.0 + scale_ref[...]) + shift_ref[...]
    h_ref[...] = h.astype(h_ref.dtype)
    small_ref[...] = _dot_f32(h, wsm_ref[...])


def _norm(x, norm_g, scale, shift, w_small, tm):
    bsz, s, d = x.shape
    return pl.pallas_call(
        _norm_kernel,
        out_shape=(jax.ShapeDtypeStruct((bsz, s, d), ACT_DTYPE),
                   jax.ShapeDtypeStruct((bsz, s, LANE), F32)),
        grid=(bsz, s // tm),
        in_specs=[pl.BlockSpec((None, tm, d), lambda b, i: (b, i, 0)),
                  pl.BlockSpec((1, d), lambda b, i: (0, 0)),
                  pl.BlockSpec((None, 1, d), lambda b, i: (b, 0, 0)),
                  pl.BlockSpec((None, 1, d), lambda b, i: (b, 0, 0)),
                  pl.BlockSpec((d, LANE), lambda b, i: (0, 0))],
        out_specs=(pl.BlockSpec((None, tm, d), lambda b, i: (b, i, 0)),
                   pl.BlockSpec((None, tm, LANE), lambda b, i: (b, i, 0))),
        compiler_params=pltpu.CompilerParams(dimension_semantics=("parallel", "parallel"),
                                             vmem_limit_bytes=VMEM_LIMIT),
        name="norm",
    )(x, norm_g.reshape(1, d), scale[:, None, :], shift[:, None, :], w_small)


def _proj_kernel(h_ref, w_ref, o_ref):
    o_ref[...] = _dot(h_ref[...], w_ref[...]).astype(o_ref.dtype)


def _proj_conv_kernel(h_ref, w_ref, cw_ref, cb_ref, o_ref, tail_ref, *, tiles_per_seq):
    i, j = pl.program_id(0), pl.program_id(1)
    tm = h_ref.shape[0]

    @pl.when(i % tiles_per_seq == 0)
    def _():
        tail_ref[j] = jnp.zeros(tail_ref.shape[1:], F32)

    acc = _dot(h_ref[...], w_ref[...])
    tail = tail_ref[j]
    x1, x2, x3 = tail[SUBLANE - 1:SUBLANE], tail[SUBLANE - 2:SUBLANE - 1], tail[SUBLANE - 3:SUBLANE - 2]
    w = [cw_ref[k:k + 1, :] for k in range(SSM_CONV)]
    first_row = lax.broadcasted_iota(jnp.int32, (SUBLANE, 1), 0) == 0

    def delay(u, u_before):
        r = pltpu.roll(u, 1, 0)
        return jnp.concatenate([jnp.where(first_row, u_before, r[:SUBLANE]), r[SUBLANE:]], axis=0)

    u = delay(w[0] * acc, w[0] * x1)
    u = delay(w[1] * acc + u, w[1] * x1 + w[0] * x2)
    u = delay(w[2] * acc + u, w[2] * x1 + w[1] * x2 + w[0] * x3)
    o_ref[...] = _silu(w[3] * acc + u + cb_ref[...]).astype(o_ref.dtype)
    tail_ref[j] = acc[tm - SUBLANE:, :]


def _proj(h2d, w, tm, tn, conv=None):
    t, d = h2d.shape
    n = w.shape[1]
    specs = [pl.BlockSpec((tm, d), lambda i, j: (i, 0)), pl.BlockSpec((d, tn), lambda i, j: (0, j))]
    args, scratch, kern, name = [h2d, w], [], _proj_kernel, "proj"
    if conv is not None:
        conv_w, conv_b, seq = conv
        assert seq % tm == 0
        specs += [pl.BlockSpec((SSM_CONV, tn), lambda i, j: (0, j)), pl.BlockSpec((1, tn), lambda i, j: (0, j))]
        args += [conv_w, conv_b.reshape(1, -1)]
        scratch = [pltpu.VMEM((n // tn, SUBLANE, tn), F32)]
        kern, name = functools.partial(_proj_conv_kernel, tiles_per_seq=seq // tm), "proj_conv"
    return pl.pallas_call(
        kern,
        out_shape=jax.ShapeDtypeStruct((t, n), ACT_DTYPE),
        grid=(t // tm, n // tn),
        in_specs=specs,
        out_specs=pl.BlockSpec((tm, tn), lambda i, j: (i, j)),
        scratch_shapes=scratch,
        compiler_params=pltpu.CompilerParams(dimension_semantics=("arbitrary", "arbitrary"),
                                             vmem_limit_bytes=VMEM_LIMIT),
        name=name,
    )(*args)


def _rope(y, cos_t, sin_t, first_half, half):
    fwd = pltpu.roll(y, LANE - half, 1)
    bwd = pltpu.roll(y, half, 1)
    return y * cos_t + jnp.where(first_half, fwd, bwd) * sin_t


def _prep_kernel(q_ref, k_ref, qi_ref, ki_ref, ca_ref, sa_ref, ci_ref, si_ref,
                 qg_ref, kg_ref, lg_ref, lb_ref, qo_ref, ko_ref, qio_ref, kio_ref):
    tm = q_ref.shape[0]
    lane = lax.broadcasted_iota(jnp.int32, (tm, LANE), 1)
    ca, sa = ca_ref[...], sa_ref[...]
    ci, si = ci_ref[...], si_ref[...]
    att_first = lane < ATT_ROT_DIM // 2
    idx_first = (lane % IDX_DIM) < IDX_ROT_DIM // 2

    def norm_rope(x, g):
        ms = jnp.mean(x * x, axis=-1, keepdims=True)
        y = x * lax.rsqrt(ms + EPS) * g
        return _rope(y, ca, sa, att_first, ATT_ROT_DIM // 2)

    q_scale = ATT_HEAD_DIM ** -0.5 * LOG2_E
    for h in range(ATT_HEADS):
        sl = slice(h * LANE, (h + 1) * LANE)
        qo_ref[:, sl] = (norm_rope(q_ref[:, sl].astype(F32), qg_ref[...]) * q_scale).astype(qo_ref.dtype)
    for h in range(ATT_KV_HEADS):
        sl = slice(h * LANE, (h + 1) * LANE)
        ko_ref[:, sl] = norm_rope(k_ref[:, sl].astype(F32), kg_ref[...]).astype(ko_ref.dtype)
    for j in range(IDX_Q_DIM // LANE):
        sl = slice(j * LANE, (j + 1) * LANE)
        y = _rope(qi_ref[:, sl].astype(F32), ci, si, idx_first, IDX_ROT_DIM // 2)
        qio_ref[:, sl] = (y * IDX_DIM ** -0.5).astype(qio_ref.dtype)
    x = ki_ref[...].astype(F32)
    real = lane < IDX_DIM
    mu = jnp.sum(jnp.where(real, x, 0.0), axis=-1, keepdims=True) * (1.0 / IDX_DIM)
    dlt = jnp.where(real, x - mu, 0.0)
    var = jnp.sum(dlt * dlt, axis=-1, keepdims=True) * (1.0 / IDX_DIM)
    y = dlt * lax.rsqrt(var + EPS) * lg_ref[...] + lb_ref[...]
    kio_ref[...] = _rope(y, ci, si, idx_first, IDX_ROT_DIM // 2).astype(kio_ref.dtype)


def _prep(proj, tabs, q_norm_g, k_norm_g, ln_g, ln_b, tm):
    t = proj.shape[0]
    ca, sa, ci, si = tabs
    row = lambda w, c0: pl.BlockSpec((tm, w), lambda i: (i, c0 // w))
    tab = pl.BlockSpec((tm, LANE), lambda i: (i, 0))
    par = pl.BlockSpec((1, LANE), lambda i: (0, 0))
    pad = lambda v: jnp.pad(v, (0, LANE - v.shape[0])).reshape(1, LANE)
    out = lambda w: pl.BlockSpec((tm, w), lambda i: (i, 0))
    return pl.pallas_call(
        _prep_kernel,
        out_shape=(jax.ShapeDtypeStruct((t, ATT_Q_DIM), MXU_DTYPE),
                   jax.ShapeDtypeStruct((t, ATT_KV_DIM), MXU_DTYPE),
                   jax.ShapeDtypeStruct((t, IDX_Q_DIM), MXU_DTYPE),
                   jax.ShapeDtypeStruct((t, LANE), MXU_DTYPE)),
        grid=(t // tm,),
        in_specs=[row(ATT_Q_DIM, COL_Q), row(ATT_KV_DIM, COL_K), row(IDX_Q_DIM, COL_QIDX), row(LANE, COL_KIDX),
                  tab, tab, tab, tab, par, par, par, par],
        out_specs=(out(ATT_Q_DIM), out(ATT_KV_DIM), out(IDX_Q_DIM), out(LANE)),
        compiler_params=pltpu.CompilerParams(dimension_semantics=("parallel",),
                                             vmem_limit_bytes=VMEM_LIMIT),
        name="prep",
    )(proj, proj, proj, proj, ca, sa, ci, si,
      q_norm_g.reshape(1, LANE), k_norm_g.reshape(1, LANE), pad(ln_g), pad(ln_b))


def _attn_kernel(q_ref, qiT_ref, smT_ref, kT_ref, v_ref, ki_ref, qg_ref, kg_ref, o_ref,
                 keys_ref, hi_ref, lo_ref, m_ref, acc_ref, qcat_ref, *, tq, topk):
    i = pl.program_id(1)
    nchunk = i + 1
    tk = tq
    w_idx = smT_ref[SMALL_WIDX:SMALL_WIDX + IDX_HEADS, :] * (IDX_HEADS ** -0.5)
    qi_all = jnp.concatenate([qiT_ref[h * IDX_DIM:(h + 1) * IDX_DIM, :] for h in range(IDX_HEADS)], axis=1)
    row_iota = lax.broadcasted_iota(jnp.int32, (tk, tq), 0)
    col_iota = lax.broadcasted_iota(jnp.int32, (tk, tq), 1)

    def score_chunk(c, diagonal):
        r0 = pl.multiple_of(c * tk, tk)
        kc = ki_ref[pl.ds(r0, tk), :][:, :IDX_DIM]
        lg = _dot(kc, qi_all)
        s = w_idx[0:1, :] * jnp.maximum(lg[:, :tq], 0.0)
        for h in range(1, IDX_HEADS):
            s = s + w_idx[h:h + 1, :] * jnp.maximum(lg[:, h * tq:(h + 1) * tq], 0.0)
        bits = lax.bitcast_convert_type(s + 0.0, jnp.int32)
        key = jnp.where(bits < 0, bits ^ jnp.int32(0x7FFFFFFF), bits)
        if diagonal:
            key = jnp.where(row_iota <= col_iota, key, jnp.int32(INT_MIN))
        keys_ref[pl.ds(r0, tk), :] = key
        hi_ref[pl.ds(r0, tk), :] = (key >> 16).astype(jnp.int16)
        lo_ref[pl.ds(r0, tk), :] = ((key & 0xFFFF) - HALF_BIAS).astype(jnp.int16)

    def score_body(c, carry):
        score_chunk(c, False)
        return carry

    lax.fori_loop(0, nchunk - 1, score_body, 0)
    score_chunk(nchunk - 1, True)

    def count(pred):
        def body(c, acc):
            kk = keys_ref[pl.ds(pl.multiple_of(c * tk, tk), tk), :]
            hit = jnp.where(pred(kk), jnp.int32(1), jnp.int32(0))
            return acc + hit.reshape(tk // SUBLANE, SUBLANE, tq).sum(axis=0)
        acc = lax.fori_loop(0, nchunk, body, jnp.zeros((SUBLANE, tq), jnp.int32))
        return acc.sum(axis=0, keepdims=True)

    pack = 2 * SUBLANE

    def count16(ref, pred):
        def body(c, acc):
            kk = ref[pl.ds(pl.multiple_of(c * tk, tk), tk), :]
            hit = jnp.where(pred(kk), jnp.bfloat16(1), jnp.bfloat16(0))
            parts = [hit[r * pack:(r + 1) * pack, :] for r in range(tk // pack)]
            while len(parts) > 1:
                parts = [parts[n] + parts[n + 1] for n in range(0, len(parts), 2)]
            return acc + parts[0].astype(F32)
        acc = lax.fori_loop(0, nchunk, body, jnp.zeros((pack, tq), F32))
        return acc.sum(axis=0, keepdims=True)

    def bisect16(ref, want):
        def bit_body(j, ans):
            trial = ans + lax.shift_left(jnp.int32(1), 15 - j)
            t16 = trial.astype(jnp.int16)
            cnt = count16(ref, lambda kk: kk >= t16)
            return jnp.where(cnt >= want, trial, ans)
        return lax.fori_loop(0, 16, bit_body, jnp.full((1, tq), -HALF_BIAS, jnp.int32))

    ans_hi = bisect16(hi_ref, float(topk))
    hi16 = ans_hi.astype(jnp.int16)
    want_lo = float(topk) - count16(hi_ref, lambda kk: kk > hi16)

    def restrict_body(c, carry):
        sl = pl.ds(pl.multiple_of(c * tk, tk), tk)
        lo_ref[sl, :] = jnp.where(hi_ref[sl, :] == hi16, lo_ref[sl, :], jnp.int16(-HALF_BIAS))
        return carry

    lax.fori_loop(0, nchunk, restrict_body, 0)
    ans_lo = bisect16(lo_ref, want_lo)
    ans = lax.shift_left(ans_hi, 16) | (ans_lo + HALF_BIAS)
    thr = jnp.maximum(ans, jnp.int32(INT_MIN + 1))
    cnt_gt = count(lambda kk: kk > thr)
    cnt_eq = count(lambda kk: kk == thr)
    need = topk - cnt_gt
    any_surplus_tie = jnp.max(jnp.where(cnt_eq > need, 1, 0)) > 0

    for g in range(ATT_KV_HEADS):
        qcat_ref[g] = jnp.concatenate(
            [q_ref[:, h * LANE:(h + 1) * LANE] for h in range(g * ATT_GROUP, (g + 1) * ATT_GROUP)], axis=0)
    acc_ref[...] = jnp.zeros(acc_ref.shape, F32)
    ones = jnp.ones((tk, LANE), MXU_DTYPE)

    bound = (ATT_HEAD_DIM ** 0.5 * LOG2_E) * jnp.max(jnp.abs(qg_ref[...])) * jnp.max(jnp.abs(kg_ref[...]))
    bounded = bound <= SAFE_LOG2_LOGIT

    def threshold_bias(c, taken):
        kk = keys_ref[pl.ds(pl.multiple_of(c * tk, tk), tk), :]
        return jnp.where(kk >= thr, 0.0, NEG_BIG).T, taken

    need_f = need.astype(F32)
    before = jnp.where(col_iota < row_iota, 1.0, 0.0).astype(jnp.bfloat16)

    def tie_bias(c, taken):
        kk = keys_ref[pl.ds(pl.multiple_of(c * tk, tk), tk), :]
        eq = kk == thr
        eq_b = jnp.where(eq, 1.0, 0.0).astype(jnp.bfloat16)
        rank = _dot(before, eq_b) + taken
        tie = jnp.where(eq, jnp.where(rank < need_f, 0.0, NEG_BIG), NEG_BIG)
        bias = jnp.where(kk > thr, 0.0, tie)
        return bias.T, taken + jnp.sum(eq_b.astype(F32), axis=0, keepdims=True)

    def masked_logits(c, g, bias_t):
        r0 = pl.multiple_of(c * tk, tk)
        s = _dot(qcat_ref[g], kT_ref[g * LANE:(g + 1) * LANE, pl.ds(r0, tk)])
        vx = jnp.concatenate([v_ref[pl.ds(r0, tk), g * LANE:(g + 1) * LANE], ones], axis=1)
        return [s[r * tq:(r + 1) * tq, :] + bias_t for r in range(ATT_GROUP)], vx

    def plain_softmax(c, bias_t):
        for g in range(ATT_KV_HEADS):
            s, vx = masked_logits(c, g, bias_t)
            p = jnp.concatenate([jnp.exp2(sr).astype(MXU_DTYPE) for sr in s], axis=0)
            acc_ref[g] += _dot(p, vx)

    def online_softmax(c, bias_t):
        for g in range(ATT_KV_HEADS):
            s, vx = masked_logits(c, g, bias_t)
            s = jnp.concatenate(s, axis=0)
            m_old = m_ref[g]
            m_new = jnp.maximum(m_old, jnp.max(s, axis=1, keepdims=True))
            p = jnp.exp2(s - m_new)
            acc_ref[g] = jnp.exp2(m_old - m_new) * acc_ref[g] + _dot(p.astype(MXU_DTYPE), vx)
            m_ref[g] = m_new

    def run(bias_fn, softmax_fn):
        def body(c, taken):
            bias_t, taken = bias_fn(c, taken)
            softmax_fn(c, bias_t)
            return taken
        lax.fori_loop(0, nchunk, body, jnp.zeros((1, tq), F32))

    @pl.when(jnp.logical_and(bounded, jnp.logical_not(any_surplus_tie)))
    def _():
        run(threshold_bias, plain_softmax)

    @pl.when(jnp.logical_and(bounded, any_surplus_tie))
    def _():
        run(tie_bias, plain_softmax)

    @pl.when(jnp.logical_not(bounded))
    def _():
        m_ref[...] = jnp.full(m_ref.shape, NEG_BIG, F32)
        run(tie_bias, online_softmax)

    for h in range(ATT_HEADS):
        g, rs = h // ATT_GROUP, slice((h % ATT_GROUP) * tq, (h % ATT_GROUP + 1) * tq)
        o_ref[:, h * LANE:(h + 1) * LANE] = (acc_ref[g, rs, :LANE] / acc_ref[g, rs, LANE:]).astype(o_ref.dtype)


def _attn(q, qiT, smallT, kT, proj3, ki, q_norm_g, k_norm_g, tq):
    bsz, s, _ = q.shape
    topk = min(TOPK_MAX, s // 4)
    kern = functools.partial(_attn_kernel, tq=tq, topk=topk)
    qtile_t = lambda r: pl.BlockSpec((None, r, tq), lambda b, i: (b, 0, i))
    par = pl.BlockSpec((1, LANE), lambda b, i: (0, 0))
    return pl.pallas_call(
        kern,
        out_shape=jax.ShapeDtypeStruct((bsz, s, ATT_Q_DIM), ACT_DTYPE),
        grid=(bsz, s // tq),
        in_specs=[pl.BlockSpec((None, tq, ATT_Q_DIM), lambda b, i: (b, i, 0)),
                  qtile_t(IDX_Q_DIM), qtile_t(LANE),
                  pl.BlockSpec((None, ATT_KV_DIM, s), lambda b, i: (b, 0, 0)),
                  pl.BlockSpec((None, s, ATT_KV_DIM), lambda b, i: (b, 0, COL_V // ATT_KV_DIM)),
                  pl.BlockSpec((None, s, LANE), lambda b, i: (b, 0, 0)), par, par],
        out_specs=pl.BlockSpec((None, tq, ATT_Q_DIM), lambda b, i: (b, i, 0)),
        scratch_shapes=[pltpu.VMEM((s, tq), jnp.int32),
                        pltpu.VMEM((s, tq), jnp.int16),
                        pltpu.VMEM((s, tq), jnp.int16),
                        pltpu.VMEM((ATT_KV_HEADS, ATT_GROUP * tq, 1), F32),
                        pltpu.VMEM((ATT_KV_HEADS, ATT_GROUP * tq, 2 * LANE), F32),
                        pltpu.VMEM((ATT_KV_HEADS, ATT_GROUP * tq, ATT_HEAD_DIM), MXU_DTYPE)],
        compiler_params=pltpu.CompilerParams(dimension_semantics=("parallel", "arbitrary"),
                                             vmem_limit_bytes=VMEM_LIMIT),
        name="attn",
    )(q, qiT, smallT, kT, proj3, ki, q_norm_g.reshape(1, LANE), k_norm_g.reshape(1, LANE))


def _ssm_kernel(xbc_ref, sm_ref, smT_ref, dtb_ref, dtbT_ref, a_ref, aT_ref,
                dsk_ref, exp_ref, y_ref, st_ref, yacc_ref):
    ln = SSM_CHUNK
    c = pl.program_id(1)

    @pl.when(c == 0)
    def _():
        st_ref[...] = jnp.zeros(st_ref.shape, F32)

    xs = xbc_ref[:, :SSM_D_INNER].astype(F32)
    b_all = xbc_ref[:, SSM_D_INNER:SSM_D_INNER + SSM_GROUPS * SSM_STATE].astype(MXU_DTYPE)
    c_all = xbc_ref[:, SSM_D_INNER + SSM_GROUPS * SSM_STATE:].astype(MXU_DTYPE)

    dt = _softplus(sm_ref[...] + dtb_ref[...])
    adt = dt * a_ref[...]
    r_i = lax.broadcasted_iota(jnp.int32, (ln, ln), 0)
    c_i = lax.broadcasted_iota(jnp.int32, (ln, ln), 1)
    tril = r_i >= c_i
    incl = jnp.where(tril, 1.0, 0.0).astype(jnp.bfloat16)
    a_col = _dot_sel_x(incl, adt)
    dtT = _softplus(smT_ref[...] + dtbT_ref[...])
    incl_t = jnp.where(r_i <= c_i, 1.0, 0.0).astype(jnp.bfloat16)
    a_row = _dot_x_sel(dtT * aT_ref[...], incl_t)

    expand = exp_ref[...]
    wide = _dot_x_sel(jnp.concatenate([a_col, dt], axis=0), expand)
    a_wide, dt_wide = wide[:ln], wide[ln:]
    a_last = a_wide[ln - 1:ln, :]
    xc = xs * dt_wide
    xc_b = xc.astype(MXU_DTYPE)
    xd_b = (xc * jnp.exp(a_last - a_wide)).astype(MXU_DTYPE)
    e_wide = jnp.exp(a_wide)
    chunk_decay = e_wide[ln - 1:ln, :]

    lane = lax.broadcasted_iota(jnp.int32, (ln, LANE), 1)
    low = lane < SSM_HEAD_DIM
    gw = SSM_D_INNER // SSM_GROUPS
    for g in range(SSM_GROUPS):
        gs = slice(g * gw, (g + 1) * gw)
        b_g = b_all[:, g * SSM_STATE:(g + 1) * SSM_STATE]
        c_g = c_all[:, g * SSM_STATE:(g + 1) * SSM_STATE]
        cb = lax.dot_general(c_g, b_g, (((1,), (1,)), ((), ())), preferred_element_type=F32)
        y_off = _dot(c_g, st_ref[:, gs].astype(MXU_DTYPE)) * e_wide[:, gs]
        st_ref[:, gs] = chunk_decay[:, gs] * st_ref[:, gs] + lax.dot_general(
            b_g, xd_b[:, gs], (((0,), (0,)), ((), ())), preferred_element_type=F32)
        yacc_ref[:, gs] = y_off
        for jp in range(gw // LANE):
            j = g * (gw // LANE) + jp
            xp = xc_b[:, j * LANE:(j + 1) * LANE]
            yd = None
            for half, keep in ((0, low), (1, jnp.logical_not(low))):
                hd = 2 * j + half
                diff = a_col[:, hd:hd + 1] - a_row[hd:hd + 1, :]
                lmat = jnp.exp(jnp.where(tril, diff, -jnp.inf))
                mh = (cb * lmat).astype(MXU_DTYPE)
                part = _dot(mh, jnp.where(keep, xp, jnp.zeros_like(xp)))
                yd = part if yd is None else yd + part
            yacc_ref[:, j * LANE:(j + 1) * LANE] += yd

    y_ref[...] = (yacc_ref[...] + dsk_ref[...] * xs).astype(y_ref.dtype)


def _ssm(xbc3, small, smallT, dt_bias, a_log, d_skip):
    bsz, s, _ = xbc3.shape
    ln = SSM_CHUNK
    padl = lambda v: jnp.pad(v, (0, LANE - v.shape[0]))
    a = padl(-jnp.exp(a_log))
    dtb = padl(dt_bias)
    head_of = jnp.arange(SSM_D_INNER) // SSM_HEAD_DIM
    expand = (jnp.arange(LANE)[:, None] == head_of[None, :]).astype(jnp.bfloat16)
    full = lambda shp: pl.BlockSpec(shp, lambda b, c: (0,) * len(shp))
    return pl.pallas_call(
        _ssm_kernel,
        out_shape=jax.ShapeDtypeStruct((bsz, s, SSM_D_INNER), ACT_DTYPE),
        grid=(bsz, s // ln),
        in_specs=[pl.BlockSpec((None, ln, SSM_CONV_DIM), lambda b, c: (b, c, 0)),
                  pl.BlockSpec((None, ln, LANE), lambda b, c: (b, c, 0)),
                  pl.BlockSpec((None, LANE, ln), lambda b, c: (b, 0, c)),
                  full((1, LANE)), full((LANE, 1)), full((1, LANE)), full((LANE, 1)),
                  full((1, SSM_D_INNER)), full((LANE, SSM_D_INNER))],
        out_specs=pl.BlockSpec((None, ln, SSM_D_INNER), lambda b, c: (b, c, 0)),
        scratch_shapes=[pltpu.VMEM((SSM_STATE, SSM_D_INNER), F32),
                        pltpu.VMEM((ln, SSM_D_INNER), F32)],
        compiler_params=pltpu.CompilerParams(dimension_semantics=("parallel", "arbitrary"),
                                             vmem_limit_bytes=VMEM_LIMIT),
        name="ssm",
    )(xbc3, small, smallT,
      dtb.reshape(1, LANE), dtb.reshape(LANE, 1), a.reshape(1, LANE), a.reshape(LANE, 1),
      jnp.repeat(d_skip, SSM_HEAD_DIM).reshape(1, -1), expand)


def _final_kernel(o_ref, za_ref, y_ref, zs_ref, ng_ref, ga_ref, gs_ref, x_ref, gate_ref, wa_ref, ws_ref, wo_ref,
                  out_ref):
    o = (o_ref[...].astype(F32) * _silu(za_ref[...].astype(F32))).astype(MXU_DTYPE)
    y_att = _dot(o, wa_ref[...])
    y = y_ref[...].astype(F32) * _silu(zs_ref[...].astype(F32))
    gw = SSM_D_INNER // SSM_GROUPS
    y_n = []
    for g in range(SSM_GROUPS):
        yg = y[:, g * gw:(g + 1) * gw]
        ms = jnp.mean(yg * yg, axis=-1, keepdims=True)
        y_n.append((yg * lax.rsqrt(ms + EPS) * ng_ref[:, g * gw:(g + 1) * gw]).astype(MXU_DTYPE))
    y_ssm = _dot(jnp.concatenate(y_n, axis=1), ws_ref[...])
    merged = _sigmoid(ga_ref[...].astype(F32)) * y_att + _sigmoid(gs_ref[...].astype(F32)) * y_ssm
    out_ref[...] = x_ref[...] + gate_ref[...] * _dot(merged.astype(MXU_DTYPE), wo_ref[...])


def _final(o3, proj3, y3, x, gate, ssm_norm_g, w_att, w_ssm, w_out, tm):
    bsz, s, d = x.shape
    tile = lambda w, c0: pl.BlockSpec((None, tm, w), lambda b, i: (b, i, c0 // w))
    full = lambda shp: pl.BlockSpec(shp, lambda b, i: (0,) * len(shp))
    return pl.pallas_call(
        _final_kernel,
        out_shape=jax.ShapeDtypeStruct((bsz, s, d), x.dtype),
        grid=(bsz, s // tm),
        in_specs=[tile(ATT_Q_DIM, 0), tile(ATT_Q_DIM, COL_ZATT), tile(SSM_D_INNER, 0),
                  tile(SSM_D_INNER, COL_ZSSM), full((1, SSM_D_INNER)),
                  tile(D_MODEL, COL_GATE), tile(D_MODEL, COL_GATE + D_MODEL), tile(d, 0),
                  pl.BlockSpec((None, 1, d), lambda b, i: (b, 0, 0)),
                  full((ATT_Q_DIM, d)), full((SSM_D_INNER, d)), full((d, d))],
        out_specs=tile(d, 0),
        compiler_params=pltpu.CompilerParams(dimension_semantics=("parallel", "parallel"),
                                             vmem_limit_bytes=VMEM_LIMIT),
        name="final",
    )(o3, proj3, y3, proj3, ssm_norm_g.reshape(1, -1), proj3, proj3, x, gate[:, None, :],
      w_att.astype(MXU_DTYPE), w_ssm.astype(MXU_DTYPE), w_out.astype(MXU_DTYPE))


def _permute_w_in(w_in):
    sizes = (ATT_Q_DIM, ATT_KV_DIM, ATT_KV_DIM, ATT_Q_DIM, IDX_Q_DIM, IDX_DIM, IDX_HEADS,
             SSM_D_INNER, SSM_CONV_DIM, SSM_HEADS, 2 * D_MODEL)
    offs = [0]
    for sz in sizes:
        offs.append(offs[-1] + sz)
    cols = lambda w, n: w[:, offs[n]:offs[n + 1]]
    d = w_in.shape[0]
    w_cast = w_in.astype(MXU_DTYPE)
    q, k, v, z_att, q_idx, k_idx, _, z_ssm, xbc, _, gate = (cols(w_cast, n) for n in range(len(sizes)))
    wide = jnp.concatenate(
        [z_ssm, gate, z_att, q, k, v, q_idx, k_idx,
         jnp.zeros((d, WIDE_COLS - COL_KIDX - IDX_DIM), MXU_DTYPE)], axis=1)
    small = jnp.concatenate([cols(w_in, 9), cols(w_in, 6), jnp.zeros((d, LANE - SSM_HEADS - IDX_HEADS), w_in.dtype)],
                            axis=1)
    return xbc, wide, small


def _rope_tables(positions):
    def angles(rot):
        inv = jnp.power(ROPE_THETA, -(jnp.arange(0, rot, 2, dtype=F32) / rot))
        ang = positions.astype(F32)[..., None] * inv
        return jnp.cos(ang), jnp.sin(ang)

    def lanes(cos, sin, width):
        half = cos.shape[-1]
        lead = cos.shape[:-1]
        ones = jnp.ones(lead + (width - 2 * half,), F32)
        reps = LANE // width
        ct = jnp.concatenate([cos, cos, ones] * reps, axis=-1)
        st = jnp.concatenate([-sin, sin, 0.0 * ones] * reps, axis=-1)
        return ct.reshape(-1, LANE), st.reshape(-1, LANE)

    ca, sa = lanes(*angles(ATT_ROT_DIM), ATT_HEAD_DIM)
    ci, si = lanes(*angles(IDX_ROT_DIM), IDX_DIM)
    return ca, sa, ci, si


def _tile(n, pref):
    return pref if n % pref == 0 else n


def kernel(x, c, positions, ada_w, ada_b, norm_g, w_in, q_norm_g, k_norm_g, idx_k_ln_g, idx_k_ln_b,
           conv_w, conv_b, dt_bias, a_log, d_skip, ssm_norm_g, w_branch_att, w_branch_ssm, w_out):
    bsz, s, d = x.shape
    t = bsz * s
    tabs = _rope_tables(positions)
    for l in range(ada_w.shape[0]):
        mod = _ada(c, ada_w[l], ada_b[l])
        shift, scale, gate = mod[:, :d], mod[:, d:2 * d], mod[:, 2 * d:]
        w_xbc, w_wide, w_small = _permute_w_in(w_in[l])
        h, small = _norm(x, norm_g[l], scale, shift, w_small, _tile(s, 512))
        tm = _tile(s, 1024)
        xbc3 = _proj(h.reshape(t, d), w_xbc, tm, 1024, conv=(conv_w[l], conv_b[l], s)).reshape(bsz, s, -1)
        proj = _proj(h.reshape(t, d), w_wide, tm, 1024)
        proj3 = proj.reshape(bsz, s, WIDE_COLS)

        q, k, q_idx, k_idx = _prep(proj, tabs, q_norm_g[l], k_norm_g[l], idx_k_ln_g[l], idx_k_ln_b[l],
                                   _tile(t, 512))
        to_t = lambda a: jnp.swapaxes(a.reshape(bsz, s, -1), 1, 2)
        smallT = jnp.swapaxes(small, 1, 2)
        o3 = _attn(q.reshape(bsz, s, -1), to_t(q_idx), smallT, to_t(k), proj3, k_idx.reshape(bsz, s, LANE),
                   q_norm_g[l], k_norm_g[l], _tile(s, 256))

        y3 = _ssm(xbc3, small, smallT, dt_bias[l], a_log[l], d_skip[l])
        x = _final(o3, proj3, y3, x, gate, ssm_norm_g[l], w_branch_att[l], w_branch_ssm[l], w_out[l], _tile(s, 512))
    return x
```

```python
import functools

import jax
import jax.numpy as jnp
from jax import lax
from jax.experimental import pallas as pl
from jax.experimental.pallas import tpu as pltpu

F32 = jnp.float32
MXU_DTYPE = jnp.bfloat16
ACT_DTYPE = jnp.bfloat16

D_MODEL = 1024
ATT_HEADS = 8
ATT_KV_HEADS = 2
ATT_HEAD_DIM = 128
ATT_GROUP = ATT_HEADS // ATT_KV_HEADS
ROPE_THETA = 500000.0
ATT_ROT_DIM = 32
IDX_HEADS = 4
IDX_DIM = 64
IDX_ROT_DIM = 16
TOPK_MAX = 256
SSM_D_INNER = 2048
SSM_HEAD_DIM = 64
SSM_HEADS = 32
SSM_GROUPS = 4
SSM_STATE = 128
SSM_CONV = 4
SSM_CHUNK = 128
SSM_CONV_DIM = SSM_D_INNER + 2 * SSM_GROUPS * SSM_STATE
EPS = 1e-6

ATT_Q_DIM = ATT_HEADS * ATT_HEAD_DIM
ATT_KV_DIM = ATT_KV_HEADS * ATT_HEAD_DIM
IDX_Q_DIM = IDX_HEADS * IDX_DIM
LANE = 128
SUBLANE = 8
INT_MIN = -(2 ** 31)
HALF_BIAS = 2 ** 15
NEG_BIG = -1e30
LOG2_E = 1.4426950408889634
SAFE_LOG2_LOGIT = 60.0
VMEM_LIMIT = 56 * 1024 * 1024

COL_ZSSM = 0
COL_GATE = COL_ZSSM + SSM_D_INNER
COL_ZATT = COL_GATE + 2 * D_MODEL
COL_Q = COL_ZATT + ATT_Q_DIM
COL_K = COL_Q + ATT_Q_DIM
COL_V = COL_K + ATT_KV_DIM
COL_QIDX = COL_V + ATT_KV_DIM
COL_KIDX = COL_QIDX + IDX_Q_DIM
WIDE_COLS = 7168
SMALL_DT = 0
SMALL_WIDX = SSM_HEADS


def _split3(x):
    hi = x.astype(jnp.bfloat16)
    r1 = x - hi.astype(F32)
    mid = r1.astype(jnp.bfloat16)
    lo = (r1 - mid.astype(F32)).astype(jnp.bfloat16)
    return hi, mid, lo


def _dot(a, b):
    return jnp.dot(a, b, preferred_element_type=F32)


def _dot_f32(a, b):
    a0, a1, a2 = _split3(a)
    b0, b1, b2 = _split3(b)
    return (_dot(a0, b0) + (_dot(a0, b1) + _dot(a1, b0))
            + (_dot(a0, b2) + _dot(a1, b1) + _dot(a2, b0)))


def _dot_x_sel(x, sel):
    m = x.shape[0]
    parts = jnp.concatenate(_split3(x), axis=0)
    r = _dot(parts, sel)
    return r[:m] + r[m:2 * m] + r[2 * m:]


def _dot_sel_x(sel, x):
    hi, mid, lo = _split3(x)
    return _dot(sel, hi) + _dot(sel, mid) + _dot(sel, lo)


def _sigmoid(x):
    return 1.0 / (1.0 + jnp.exp(-x))


def _silu(x):
    return x * _sigmoid(x)


def _softplus(x):
    return jnp.maximum(x, 0.0) + jnp.log(1.0 + jnp.exp(-jnp.abs(x)))


def _ada_kernel(c_ref, w_ref, b_ref, o_ref):
    o_ref[...] = _dot_f32(_silu(c_ref[...]), w_ref[...]) + b_ref[...]


def _ada(c, ada_w, ada_b):
    bsz = c.shape[0]
    return pl.pallas_call(
        _ada_kernel,
        out_shape=jax.ShapeDtypeStruct((bsz, 3 * D_MODEL), F32),
        grid=(3,),
        in_specs=[pl.BlockSpec((bsz, D_MODEL), lambda j: (0, 0)),
                  pl.BlockSpec((D_MODEL, D_MODEL), lambda j: (0, j)),
                  pl.BlockSpec((1, D_MODEL), lambda j: (0, j))],
        out_specs=pl.BlockSpec((bsz, D_MODEL), lambda j: (0, j)),
        compiler_params=pltpu.CompilerParams(dimension_semantics=("arbitrary",),
                                             vmem_limit_bytes=VMEM_LIMIT),
        name="ada",
    )(c, ada_w, ada_b.reshape(1, -1))


def _norm_kernel(x_ref, g_ref, scale_ref, shift_ref, wsm_ref, h_ref, small_ref):
    x = x_ref[...]
    ms = jnp.mean(x * x, axis=-1, keepdims=True)
    y = x * lax.rsqrt(ms + EPS) * g_ref[...]
    h = y * (1.0 + scale_ref[...]) + shift_ref[...]
    h_ref[...] = h.astype(h_ref.dtype)
    small_ref[...] = _dot_f32(h, wsm_ref[...])


def _norm(x, norm_g, scale, shift, w_small, tm):
    bsz, s, d = x.shape
    return pl.pallas_call(
        _norm_kernel,
        out_shape=(jax.ShapeDtypeStruct((bsz, s, d), ACT_DTYPE),
                   jax.ShapeDtypeStruct((bsz, s, LANE), F32)),
        grid=(bsz, s // tm),
        in_specs=[pl.BlockSpec((None, tm, d), lambda b, i: (b, i, 0)),
                  pl.BlockSpec((1, d), lambda b, i: (0, 0)),
                  pl.BlockSpec((None, 1, d), lambda b, i: (b, 0, 0)),
                  pl.BlockSpec((None, 1, d), lambda b, i: (b, 0, 0)),
                  pl.BlockSpec((d, LANE), lambda b, i: (0, 0))],
        out_specs=(pl.BlockSpec((None, tm, d), lambda b, i: (b, i, 0)),
                   pl.BlockSpec((None, tm, LANE), lambda b, i: (b, i, 0))),
        compiler_params=pltpu.CompilerParams(dimension_semantics=("parallel", "parallel"),
                                             vmem_limit_bytes=VMEM_LIMIT),
        name="norm",
    )(x, norm_g.reshape(1, d), scale[:, None, :], shift[:, None, :], w_small)


def _proj_kernel(h_ref, w_ref, o_ref):
    o_ref[...] = _dot(h_ref[...], w_ref[...]).astype(o_ref.dtype)


def _proj_conv_kernel(h_ref, w_ref, cw_ref, cb_ref, o_ref, tail_ref, *, tiles_per_seq):
    i, j = pl.program_id(0), pl.program_id(1)
    tm = h_ref.shape[0]

    @pl.when(i % tiles_per_seq == 0)
    def _():
        tail_ref[j] = jnp.zeros(tail_ref.shape[1:], F32)

    acc = _dot(h_ref[...], w_ref[...])
    tail = tail_ref[j]
    x1, x2, x3 = tail[SUBLANE - 1:SUBLANE], tail[SUBLANE - 2:SUBLANE - 1], tail[SUBLANE - 3:SUBLANE - 2]
    w = [cw_ref[k:k + 1, :] for k in range(SSM_CONV)]
    first_row = lax.broadcasted_iota(jnp.int32, (SUBLANE, 1), 0) == 0

    def delay(u, u_before):
        r = pltpu.roll(u, 1, 0)
        return jnp.concatenate([jnp.where(first_row, u_before, r[:SUBLANE]), r[SUBLANE:]], axis=0)

    u = delay(w[0] * acc, w[0] * x1)
    u = delay(w[1] * acc + u, w[1] * x1 + w[0] * x2)
    u = delay(w[2] * acc + u, w[2] * x1 + w[1] * x2 + w[0] * x3)
    o_ref[...] = _silu(w[3] * acc + u + cb_ref[...]).astype(o_ref.dtype)
    tail_ref[j] = acc[tm - SUBLANE:, :]


def _proj(h2d, w, tm, tn, conv=None):
    t, d = h2d.shape
    n = w.shape[1]
    specs = [pl.BlockSpec((tm, d), lambda i, j: (i, 0)), pl.BlockSpec((d, tn), lambda i, j: (0, j))]
    args, scratch, kern, name = [h2d, w], [], _proj_kernel, "proj"
    if conv is not None:
        conv_w, conv_b, seq = conv
        assert seq % tm == 0
        specs += [pl.BlockSpec((SSM_CONV, tn), lambda i, j: (0, j)), pl.BlockSpec((1, tn), lambda i, j: (0, j))]
        args += [conv_w, conv_b.reshape(1, -1)]
        scratch = [pltpu.VMEM((n // tn, SUBLANE, tn), F32)]
        kern, name = functools.partial(_proj_conv_kernel, tiles_per_seq=seq // tm), "proj_conv"
    return pl.pallas_call(
        kern,
        out_shape=jax.ShapeDtypeStruct((t, n), ACT_DTYPE),
        grid=(t // tm, n // tn),
        in_specs=specs,
        out_specs=pl.BlockSpec((tm, tn), lambda i, j: (i, j)),
        scratch_shapes=scratch,
        compiler_params=pltpu.CompilerParams(dimension_semantics=("arbitrary", "arbitrary"),
                                             vmem_limit_bytes=VMEM_LIMIT),
        name=name,
    )(*args)


def _rope(y, cos_t, sin_t, first_half, half):
    fwd = pltpu.roll(y, LANE - half, 1)
    bwd = pltpu.roll(y, half, 1)
    return y * cos_t + jnp.where(first_half, fwd, bwd) * sin_t


def _prep_kernel(q_ref, k_ref, qi_ref, ki_ref, pos_ref, inv_ref, exp_ref, pass_ref,
                 qg_ref, kg_ref, lg_ref, lb_ref, qo_ref, ko_ref, qio_ref, kio_ref):
    tm = q_ref.shape[0]
    lane = lax.broadcasted_iota(jnp.int32, (tm, LANE), 1)
    ang = pos_ref[...].astype(F32) * inv_ref[...]
    cos_v, sin_v = jnp.cos(ang), jnp.sin(ang)
    ca = _dot_x_sel(cos_v, exp_ref[0]) + pass_ref[0:1, :]
    sa = _dot_x_sel(sin_v, exp_ref[1])
    ci = _dot_x_sel(cos_v, exp_ref[2]) + pass_ref[1:2, :]
    si = _dot_x_sel(sin_v, exp_ref[3])
    att_first = lane < ATT_ROT_DIM // 2
    idx_first = (lane % IDX_DIM) < IDX_ROT_DIM // 2

    def norm_rope(x, g):
        ms = jnp.mean(x * x, axis=-1, keepdims=True)
        y = x * lax.rsqrt(ms + EPS) * g
        return _rope(y, ca, sa, att_first, ATT_ROT_DIM // 2)

    q_scale = ATT_HEAD_DIM ** -0.5 * LOG2_E
    for h in range(ATT_HEADS):
        sl = slice(h * LANE, (h + 1) * LANE)
        qo_ref[:, sl] = (norm_rope(q_ref[:, sl].astype(F32), qg_ref[...]) * q_scale).astype(qo_ref.dtype)
    for h in range(ATT_KV_HEADS):
        sl = slice(h * LANE, (h + 1) * LANE)
        ko_ref[:, sl] = norm_rope(k_ref[:, sl].astype(F32), kg_ref[...]).astype(ko_ref.dtype)
    for j in range(IDX_Q_DIM // LANE):
        sl = slice(j * LANE, (j + 1) * LANE)
        y = _rope(qi_ref[:, sl].astype(F32), ci, si, idx_first, IDX_ROT_DIM // 2)
        qio_ref[:, sl] = (y * IDX_DIM ** -0.5).astype(qio_ref.dtype)
    x = ki_ref[...].astype(F32)
    real = lane < IDX_DIM
    mu = jnp.sum(jnp.where(real, x, 0.0), axis=-1, keepdims=True) * (1.0 / IDX_DIM)
    dlt = jnp.where(real, x - mu, 0.0)
    var = jnp.sum(dlt * dlt, axis=-1, keepdims=True) * (1.0 / IDX_DIM)
    y = dlt * lax.rsqrt(var + EPS) * lg_ref[...] + lb_ref[...]
    kio_ref[...] = _rope(y, ci, si, idx_first, IDX_ROT_DIM // 2).astype(kio_ref.dtype)


def _rope_constants():
    def inv_freq(rot):
        return jnp.power(ROPE_THETA, -(jnp.arange(0, rot, 2, dtype=F32) / rot))

    half_a, half_i = ATT_ROT_DIM // 2, IDX_ROT_DIM // 2
    inv = jnp.zeros((LANE,), F32).at[:half_a].set(inv_freq(ATT_ROT_DIM)).at[half_a:half_a + half_i].set(
        inv_freq(IDX_ROT_DIM))
    src = jnp.arange(LANE)[:, None]
    dst = jnp.arange(LANE)[None, :]
    a_lo = (dst < half_a) & (src == dst)
    a_hi = (dst >= half_a) & (dst < 2 * half_a) & (src == dst - half_a)
    d64 = dst % IDX_DIM
    i_lo = (d64 < half_i) & (src == half_a + d64)
    i_hi = (d64 >= half_i) & (d64 < 2 * half_i) & (src == half_a + d64 - half_i)
    as_f = lambda m: m.astype(F32)
    spread = jnp.stack([as_f(a_lo) + as_f(a_hi), as_f(a_hi) - as_f(a_lo),
                        as_f(i_lo) + as_f(i_hi), as_f(i_hi) - as_f(i_lo)]).astype(jnp.bfloat16)
    lanes = jnp.arange(LANE)
    passthrough = jnp.stack([as_f(lanes >= 2 * half_a), as_f(lanes % IDX_DIM >= 2 * half_i)])
    return inv.reshape(1, LANE), spread, passthrough


def _prep(proj, positions, q_norm_g, k_norm_g, ln_g, ln_b, tm):
    t = proj.shape[0]
    inv, spread, passthrough = _rope_constants()
    row = lambda w, c0: pl.BlockSpec((tm, w), lambda i: (i, c0 // w))
    par = pl.BlockSpec((1, LANE), lambda i: (0, 0))
    pad = lambda v: jnp.pad(v, (0, LANE - v.shape[0])).reshape(1, LANE)
    out = lambda w: pl.BlockSpec((tm, w), lambda i: (i, 0))
    return pl.pallas_call(
        _prep_kernel,
        out_shape=(jax.ShapeDtypeStruct((t, ATT_Q_DIM), MXU_DTYPE),
                   jax.ShapeDtypeStruct((t, ATT_KV_DIM), MXU_DTYPE),
                   jax.ShapeDtypeStruct((t, IDX_Q_DIM), MXU_DTYPE),
                   jax.ShapeDtypeStruct((t, LANE), MXU_DTYPE)),
        grid=(t // tm,),
        in_specs=[row(ATT_Q_DIM, COL_Q), row(ATT_KV_DIM, COL_K), row(IDX_Q_DIM, COL_QIDX), row(LANE, COL_KIDX),
                  pl.BlockSpec((tm, 1), lambda i: (i, 0)), par,
                  pl.BlockSpec(spread.shape, lambda i: (0, 0, 0)), pl.BlockSpec(passthrough.shape, lambda i: (0, 0)),
                  par, par, par, par],
        out_specs=(out(ATT_Q_DIM), out(ATT_KV_DIM), out(IDX_Q_DIM), out(LANE)),
        compiler_params=pltpu.CompilerParams(dimension_semantics=("parallel",),
                                             vmem_limit_bytes=VMEM_LIMIT),
        name="prep",
    )(proj, proj, proj, proj, positions.reshape(t, 1), inv, spread, passthrough,
      q_norm_g.reshape(1, LANE), k_norm_g.reshape(1, LANE), pad(ln_g), pad(ln_b))


def _attn_kernel(q_ref, qiT_ref, smT_ref, kT_ref, v_ref, ki_ref, qg_ref, kg_ref, o_ref,
                 keys_ref, hi_ref, lo_ref, m_ref, acc_ref, qcat_ref, *, tq, topk):
    i = pl.program_id(1)
    nchunk = i + 1
    tk = tq
    w_idx = smT_ref[SMALL_WIDX:SMALL_WIDX + IDX_HEADS, :] * (IDX_HEADS ** -0.5)
    qi_all = jnp.concatenate([qiT_ref[h * IDX_DIM:(h + 1) * IDX_DIM, :] for h in range(IDX_HEADS)], axis=1)
    row_iota = lax.broadcasted_iota(jnp.int32, (tk, tq), 0)
    col_iota = lax.broadcasted_iota(jnp.int32, (tk, tq), 1)

    def score_chunk(c, diagonal):
        r0 = pl.multiple_of(c * tk, tk)
        kc = ki_ref[pl.ds(r0, tk), :][:, :IDX_DIM]
        lg = _dot(kc, qi_all)
        s = w_idx[0:1, :] * jnp.maximum(lg[:, :tq], 0.0)
        for h in range(1, IDX_HEADS):
            s = s + w_idx[h:h + 1, :] * jnp.maximum(lg[:, h * tq:(h + 1) * tq], 0.0)
        bits = lax.bitcast_convert_type(s + 0.0, jnp.int32)
        key = jnp.where(bits < 0, bits ^ jnp.int32(0x7FFFFFFF), bits)
        if diagonal:
            key = jnp.where(row_iota <= col_iota, key, jnp.int32(INT_MIN))
        keys_ref[pl.ds(r0, tk), :] = key
        hi_ref[pl.ds(r0, tk), :] = (key >> 16).astype(jnp.int16)
        lo_ref[pl.ds(r0, tk), :] = ((key & 0xFFFF) - HALF_BIAS).astype(jnp.int16)

    def score_body(c, carry):
        score_chunk(c, False)
        return carry

    lax.fori_loop(0, nchunk - 1, score_body, 0)
    score_chunk(nchunk - 1, True)

    pack = 2 * SUBLANE

    def count16(ref, pred):
        def body(c, acc):
            kk = ref[pl.ds(pl.multiple_of(c * tk, tk), tk), :]
            hit = jnp.where(pred(kk), jnp.bfloat16(1), jnp.bfloat16(0))
            parts = [hit[r * pack:(r + 1) * pack, :] for r in range(tk // pack)]
            while len(parts) > 1:
                parts = [parts[n] + parts[n + 1] for n in range(0, len(parts), 2)]
            return acc + parts[0].astype(F32)
        acc = lax.fori_loop(0, nchunk, body, jnp.zeros((pack, tq), F32))
        return acc.sum(axis=0, keepdims=True)

    def bisect16(ref, want):
        def bit_body(j, ans):
            trial = ans + lax.shift_left(jnp.int32(1), 15 - j)
            t16 = trial.astype(jnp.int16)
            cnt = count16(ref, lambda kk: kk >= t16)
            return jnp.where(cnt >= want, trial, ans)
        return lax.fori_loop(0, 16, bit_body, jnp.full((1, tq), -HALF_BIAS, jnp.int32))

    ans_hi = bisect16(hi_ref, float(topk))
    hi16 = ans_hi.astype(jnp.int16)
    want_lo = float(topk) - count16(hi_ref, lambda kk: kk > hi16)

    def restrict_body(c, carry):
        sl = pl.ds(pl.multiple_of(c * tk, tk), tk)
        lo_ref[sl, :] = jnp.where(hi_ref[sl, :] == hi16, lo_ref[sl, :], jnp.int16(-HALF_BIAS))
        return carry

    lax.fori_loop(0, nchunk, restrict_body, 0)
    ans_lo = bisect16(lo_ref, want_lo)
    ans = lax.shift_left(ans_hi, 16) | (ans_lo + HALF_BIAS)
    thr = jnp.maximum(ans, jnp.int32(INT_MIN + 1))
    lo16 = ans_lo.astype(jnp.int16)
    need_f = want_lo - count16(lo_ref, lambda kk: kk > lo16)

    for g in range(ATT_KV_HEADS):
        qcat_ref[g] = jnp.concatenate(
            [q_ref[:, h * LANE:(h + 1) * LANE] for h in range(g * ATT_GROUP, (g + 1) * ATT_GROUP)], axis=0)
    acc_ref[...] = jnp.zeros(acc_ref.shape, F32)
    ones = jnp.ones((tk, LANE), MXU_DTYPE)

    bound = (ATT_HEAD_DIM ** 0.5 * LOG2_E) * jnp.max(jnp.abs(qg_ref[...])) * jnp.max(jnp.abs(kg_ref[...]))
    bounded = bound <= SAFE_LOG2_LOGIT

    before = jnp.where(col_iota < row_iota, 1.0, 0.0).astype(jnp.bfloat16)

    def tie_bias(c, taken):
        kk = keys_ref[pl.ds(pl.multiple_of(c * tk, tk), tk), :]
        eq = kk == thr
        eq_b = jnp.where(eq, 1.0, 0.0).astype(jnp.bfloat16)
        rank = _dot(before, eq_b) + taken
        tie = jnp.where(eq, jnp.where(rank < need_f, 0.0, NEG_BIG), NEG_BIG)
        bias = jnp.where(kk > thr, 0.0, tie)
        return bias.T, taken + jnp.sum(eq_b.astype(F32), axis=0, keepdims=True)

    def masked_logits(c, g, bias_t):
        r0 = pl.multiple_of(c * tk, tk)
        s = _dot(qcat_ref[g], kT_ref[g * LANE:(g + 1) * LANE, pl.ds(r0, tk)])
        vx = jnp.concatenate([v_ref[pl.ds(r0, tk), g * LANE:(g + 1) * LANE], ones], axis=1)
        return [s[r * tq:(r + 1) * tq, :] + bias_t for r in range(ATT_GROUP)], vx

    def plain_softmax(c, bias_t):
        for g in range(ATT_KV_HEADS):
            s, vx = masked_logits(c, g, bias_t)
            p = jnp.concatenate([jnp.exp2(sr).astype(MXU_DTYPE) for sr in s], axis=0)
            acc_ref[g] += _dot(p, vx)

    def online_softmax(c, bias_t):
        for g in range(ATT_KV_HEADS):
            s, vx = masked_logits(c, g, bias_t)
            s = jnp.concatenate(s, axis=0)
            m_old = m_ref[g]
            m_new = jnp.maximum(m_old, jnp.max(s, axis=1, keepdims=True))
            p = jnp.exp2(s - m_new)
            acc_ref[g] = jnp.exp2(m_old - m_new) * acc_ref[g] + _dot(p.astype(MXU_DTYPE), vx)
            m_ref[g] = m_new

    def run(bias_fn, softmax_fn):
        def body(c, taken):
            bias_t, taken = bias_fn(c, taken)
            softmax_fn(c, bias_t)
            return taken
        lax.fori_loop(0, nchunk, body, jnp.zeros((1, tq), F32))

    @pl.when(bounded)
    def _():
        run(tie_bias, plain_softmax)

    @pl.when(jnp.logical_not(bounded))
    def _():
        m_ref[...] = jnp.full(m_ref.shape, NEG_BIG, F32)
        run(tie_bias, online_softmax)

    for h in range(ATT_HEADS):
        g, rs = h // ATT_GROUP, slice((h % ATT_GROUP) * tq, (h % ATT_GROUP + 1) * tq)
        o_ref[:, h * LANE:(h + 1) * LANE] = (acc_ref[g, rs, :LANE] / acc_ref[g, rs, LANE:]).astype(o_ref.dtype)


def _attn(q, qiT, smallT, kT, proj3, ki, q_norm_g, k_norm_g, tq):
    bsz, s, _ = q.shape
    topk = min(TOPK_MAX, s // 4)
    kern = functools.partial(_attn_kernel, tq=tq, topk=topk)
    qtile_t = lambda r: pl.BlockSpec((None, r, tq), lambda b, i: (b, 0, i))
    par = pl.BlockSpec((1, LANE), lambda b, i: (0, 0))
    return pl.pallas_call(
        kern,
        out_shape=jax.ShapeDtypeStruct((bsz, s, ATT_Q_DIM), ACT_DTYPE),
        grid=(bsz, s // tq),
        in_specs=[pl.BlockSpec((None, tq, ATT_Q_DIM), lambda b, i: (b, i, 0)),
                  qtile_t(IDX_Q_DIM), qtile_t(LANE),
                  pl.BlockSpec((None, ATT_KV_DIM, s), lambda b, i: (b, 0, 0)),
                  pl.BlockSpec((None, s, ATT_KV_DIM), lambda b, i: (b, 0, COL_V // ATT_KV_DIM)),
                  pl.BlockSpec((None, s, LANE), lambda b, i: (b, 0, 0)), par, par],
        out_specs=pl.BlockSpec((None, tq, ATT_Q_DIM), lambda b, i: (b, i, 0)),
        scratch_shapes=[pltpu.VMEM((s, tq), jnp.int32),
                        pltpu.VMEM((s, tq), jnp.int16),
                        pltpu.VMEM((s, tq), jnp.int16),
                        pltpu.VMEM((ATT_KV_HEADS, ATT_GROUP * tq, 1), F32),
                        pltpu.VMEM((ATT_KV_HEADS, ATT_GROUP * tq, 2 * LANE), F32),
                        pltpu.VMEM((ATT_KV_HEADS, ATT_GROUP * tq, ATT_HEAD_DIM), MXU_DTYPE)],
        compiler_params=pltpu.CompilerParams(dimension_semantics=("parallel", "arbitrary"),
                                             vmem_limit_bytes=VMEM_LIMIT),
        name="attn",
    )(q, qiT, smallT, kT, proj3, ki, q_norm_g.reshape(1, LANE), k_norm_g.reshape(1, LANE))


def _ssm_kernel(xbc_ref, sm_ref, smT_ref, dtb_ref, dtbT_ref, a_ref, aT_ref,
                dsk_ref, exp_ref, y_ref, st_ref, yacc_ref):
    ln = SSM_CHUNK
    c = pl.program_id(1)

    @pl.when(c == 0)
    def _():
        st_ref[...] = jnp.zeros(st_ref.shape, F32)

    xs = xbc_ref[:, :SSM_D_INNER].astype(F32)
    b_all = xbc_ref[:, SSM_D_INNER:SSM_D_INNER + SSM_GROUPS * SSM_STATE].astype(MXU_DTYPE)
    c_all = xbc_ref[:, SSM_D_INNER + SSM_GROUPS * SSM_STATE:].astype(MXU_DTYPE)

    dt = _softplus(sm_ref[...] + dtb_ref[...])
    adt = dt * a_ref[...]
    r_i = lax.broadcasted_iota(jnp.int32, (ln, ln), 0)
    c_i = lax.broadcasted_iota(jnp.int32, (ln, ln), 1)
    tril = r_i >= c_i
    incl = jnp.where(tril, 1.0, 0.0).astype(jnp.bfloat16)
    a_col = _dot_sel_x(incl, adt)
    dtT = _softplus(smT_ref[...] + dtbT_ref[...])
    incl_t = jnp.where(r_i <= c_i, 1.0, 0.0).astype(jnp.bfloat16)
    a_row = _dot_x_sel(dtT * aT_ref[...], incl_t)

    expand = exp_ref[...]
    wide = _dot_x_sel(jnp.concatenate([a_col, dt], axis=0), expand)
    a_wide, dt_wide = wide[:ln], wide[ln:]
    a_last = a_wide[ln - 1:ln, :]
    xc = xs * dt_wide
    xc_b = xc.astype(MXU_DTYPE)
    xd_b = (xc * jnp.exp(a_last - a_wide)).astype(MXU_DTYPE)
    e_wide = jnp.exp(a_wide)
    chunk_decay = e_wide[ln - 1:ln, :]

    lane = lax.broadcasted_iota(jnp.int32, (ln, LANE), 1)
    low = lane < SSM_HEAD_DIM
    gw = SSM_D_INNER // SSM_GROUPS
    for g in range(SSM_GROUPS):
        gs = slice(g * gw, (g + 1) * gw)
        b_g = b_all[:, g * SSM_STATE:(g + 1) * SSM_STATE]
        c_g = c_all[:, g * SSM_STATE:(g + 1) * SSM_STATE]
        cb = lax.dot_general(c_g, b_g, (((1,), (1,)), ((), ())), preferred_element_type=F32)
        y_off = _dot(c_g, st_ref[:, gs].astype(MXU_DTYPE)) * e_wide[:, gs]
        st_ref[:, gs] = chunk_decay[:, gs] * st_ref[:, gs] + lax.dot_general(
            b_g, xd_b[:, gs], (((0,), (0,)), ((), ())), preferred_element_type=F32)
        yacc_ref[:, gs] = y_off
        for jp in range(gw // LANE):
            j = g * (gw // LANE) + jp
            xp = xc_b[:, j * LANE:(j + 1) * LANE]
            yd = None
            for half, keep in ((0, low), (1, jnp.logical_not(low))):
                hd = 2 * j + half
                diff = a_col[:, hd:hd + 1] - a_row[hd:hd + 1, :]
                lmat = jnp.exp(jnp.where(tril, diff, -jnp.inf))
                mh = (cb * lmat).astype(MXU_DTYPE)
                part = _dot(mh, jnp.where(keep, xp, jnp.zeros_like(xp)))
                yd = part if yd is None else yd + part
            yacc_ref[:, j * LANE:(j + 1) * LANE] += yd

    y_ref[...] = (yacc_ref[...] + dsk_ref[...] * xs).astype(y_ref.dtype)


def _ssm(xbc3, small, smallT, dt_bias, a_log, d_skip):
    bsz, s, _ = xbc3.shape
    ln = SSM_CHUNK
    padl = lambda v: jnp.pad(v, (0, LANE - v.shape[0]))
    a = padl(-jnp.exp(a_log))
    dtb = padl(dt_bias)
    head_of = jnp.arange(SSM_D_INNER) // SSM_HEAD_DIM
    expand = (jnp.arange(LANE)[:, None] == head_of[None, :]).astype(jnp.bfloat16)
    full = lambda shp: pl.BlockSpec(shp, lambda b, c: (0,) * len(shp))
    return pl.pallas_call(
        _ssm_kernel,
        out_shape=jax.ShapeDtypeStruct((bsz, s, SSM_D_INNER), ACT_DTYPE),
        grid=(bsz, s // ln),
        in_specs=[pl.BlockSpec((None, ln, SSM_CONV_DIM), lambda b, c: (b, c, 0)),
                  pl.BlockSpec((None, ln, LANE), lambda b, c: (b, c, 0)),
                  pl.BlockSpec((None, LANE, ln), lambda b, c: (b, 0, c)),
                  full((1, LANE)), full((LANE, 1)), full((1, LANE)), full((LANE, 1)),
                  full((1, SSM_D_INNER)), full((LANE, SSM_D_INNER))],
        out_specs=pl.BlockSpec((None, ln, SSM_D_INNER), lambda b, c: (b, c, 0)),
        scratch_shapes=[pltpu.VMEM((SSM_STATE, SSM_D_INNER), F32),
                        pltpu.VMEM((ln, SSM_D_INNER), F32)],
        compiler_params=pltpu.CompilerParams(dimension_semantics=("parallel", "arbitrary"),
                                             vmem_limit_bytes=VMEM_LIMIT),
        name="ssm",
    )(xbc3, small, smallT,
      dtb.reshape(1, LANE), dtb.reshape(LANE, 1), a.reshape(1, LANE), a.reshape(LANE, 1),
      jnp.repeat(d_skip, SSM_HEAD_DIM).reshape(1, -1), expand)


def _final_kernel(o_ref, za_ref, y_ref, zs_ref, ng_ref, ga_ref, gs_ref, x_ref, gate_ref, wa_ref, ws_ref, wo_ref,
                  out_ref):
    o = (o_ref[...].astype(F32) * _silu(za_ref[...].astype(F32))).astype(MXU_DTYPE)
    y_att = _dot(o, wa_ref[...])
    y = y_ref[...].astype(F32) * _silu(zs_ref[...].astype(F32))
    gw = SSM_D_INNER // SSM_GROUPS
    y_n = []
    for g in range(SSM_GROUPS):
        yg = y[:, g * gw:(g + 1) * gw]
        ms = jnp.mean(yg * yg, axis=-1, keepdims=True)
        y_n.append((yg * lax.rsqrt(ms + EPS) * ng_ref[:, g * gw:(g + 1) * gw]).astype(MXU_DTYPE))
    y_ssm = _dot(jnp.concatenate(y_n, axis=1), ws_ref[...])
    merged = _sigmoid(ga_ref[...].astype(F32)) * y_att + _sigmoid(gs_ref[...].astype(F32)) * y_ssm
    out_ref[...] = x_ref[...] + gate_ref[...] * _dot(merged.astype(MXU_DTYPE), wo_ref[...])


def _final(o3, proj3, y3, x, gate, ssm_norm_g, w_att, w_ssm, w_out, tm):
    bsz, s, d = x.shape
    tile = lambda w, c0: pl.BlockSpec((None, tm, w), lambda b, i: (b, i, c0 // w))
    full = lambda shp: pl.BlockSpec(shp, lambda b, i: (0,) * len(shp))
    return pl.pallas_call(
        _final_kernel,
        out_shape=jax.ShapeDtypeStruct((bsz, s, d), x.dtype),
        grid=(bsz, s // tm),
        in_specs=[tile(ATT_Q_DIM, 0), tile(ATT_Q_DIM, COL_ZATT), tile(SSM_D_INNER, 0),
                  tile(SSM_D_INNER, COL_ZSSM), full((1, SSM_D_INNER)),
                  tile(D_MODEL, COL_GATE), tile(D_MODEL, COL_GATE + D_MODEL), tile(d, 0),
                  pl.BlockSpec((None, 1, d), lambda b, i: (b, 0, 0)),
                  full((ATT_Q_DIM, d)), full((SSM_D_INNER, d)), full((d, d))],
        out_specs=tile(d, 0),
        compiler_params=pltpu.CompilerParams(dimension_semantics=("parallel", "parallel"),
                                             vmem_limit_bytes=VMEM_LIMIT),
        name="final",
    )(o3, proj3, y3, proj3, ssm_norm_g.reshape(1, -1), proj3, proj3, x, gate[:, None, :],
      w_att.astype(MXU_DTYPE), w_ssm.astype(MXU_DTYPE), w_out.astype(MXU_DTYPE))


def _permute_w_in(w_in):
    sizes = (ATT_Q_DIM, ATT_KV_DIM, ATT_KV_DIM, ATT_Q_DIM, IDX_Q_DIM, IDX_DIM, IDX_HEADS,
             SSM_D_INNER, SSM_CONV_DIM, SSM_HEADS, 2 * D_MODEL)
    offs = [0]
    for sz in sizes:
        offs.append(offs[-1] + sz)
    cols = lambda w, n: w[:, offs[n]:offs[n + 1]]
    d = w_in.shape[0]
    w_cast = w_in.astype(MXU_DTYPE)
    q, k, v, z_att, q_idx, k_idx, _, z_ssm, xbc, _, gate = (cols(w_cast, n) for n in range(len(sizes)))
    wide = jnp.concatenate(
        [z_ssm, gate, z_att, q, k, v, q_idx, k_idx,
         jnp.zeros((d, WIDE_COLS - COL_KIDX - IDX_DIM), MXU_DTYPE)], axis=1)
    small = jnp.concatenate([cols(w_in, 9), cols(w_in, 6), jnp.zeros((d, LANE - SSM_HEADS - IDX_HEADS), w_in.dtype)],
                            axis=1)
    return xbc, wide, small


def _tile(n, pref):
    return pref if n % pref == 0 else n


def kernel(x, c, positions, ada_w, ada_b, norm_g, w_in, q_norm_g, k_norm_g, idx_k_ln_g, idx_k_ln_b,
           conv_w, conv_b, dt_bias, a_log, d_skip, ssm_norm_g, w_branch_att, w_branch_ssm, w_out):
    bsz, s, d = x.shape
    t = bsz * s
    for l in range(ada_w.shape[0]):
        mod = _ada(c, ada_w[l], ada_b[l])
        shift, scale, gate = mod[:, :d], mod[:, d:2 * d], mod[:, 2 * d:]
        w_xbc, w_wide, w_small = _permute_w_in(w_in[l])
        h, small = _norm(x, norm_g[l], scale, shift, w_small, _tile(s, 512))
        tm = _tile(s, 1024)
        xbc3 = _proj(h.reshape(t, d), w_xbc, tm, 1024, conv=(conv_w[l], conv_b[l], s)).reshape(bsz, s, -1)
        proj = _proj(h.reshape(t, d), w_wide, tm, 1024)
        proj3 = proj.reshape(bsz, s, WIDE_COLS)

        q, k, q_idx, k_idx = _prep(proj, positions, q_norm_g[l], k_norm_g[l], idx_k_ln_g[l], idx_k_ln_b[l],
                                   _tile(t, 512))
        to_t = lambda a: jnp.swapaxes(a.reshape(bsz, s, -1), 1, 2)
        smallT = jnp.swapaxes(small, 1, 2)
        o3 = _attn(q.reshape(bsz, s, -1), to_t(q_idx), smallT, to_t(k), proj3, k_idx.reshape(bsz, s, LANE),
                   q_norm_g[l], k_norm_g[l], _tile(s, 256))

        y3 = _ssm(xbc3, small, smallT, dt_bias[l], a_log[l], d_skip[l])
        x = _final(o3, proj3, y3, x, gate, ssm_norm_g[l], w_branch_att[l], w_branch_ssm[l], w_out[l], _tile(s, 512))
    return x
```

```python
import functools

import jax
import jax.numpy as jnp
from jax import lax
from jax.experimental import pallas as pl
from jax.experimental.pallas import tpu as pltpu

F32 = jnp.float32
MXU_DTYPE = jnp.bfloat16
ACT_DTYPE = jnp.bfloat16

D_MODEL = 1024
ATT_HEADS = 8
ATT_KV_HEADS = 2
ATT_HEAD_DIM = 128
ATT_GROUP = ATT_HEADS // ATT_KV_HEADS
ROPE_THETA = 500000.0
ATT_ROT_DIM = 32
IDX_HEADS = 4
IDX_DIM = 64
IDX_ROT_DIM = 16
TOPK_MAX = 256
SSM_D_INNER = 2048
SSM_HEAD_DIM = 64
SSM_HEADS = 32
SSM_GROUPS = 4
SSM_STATE = 128
SSM_CONV = 4
SSM_CHUNK = 128
SSM_CONV_DIM = SSM_D_INNER + 2 * SSM_GROUPS * SSM_STATE
EPS = 1e-6

ATT_Q_DIM = ATT_HEADS * ATT_HEAD_DIM
ATT_KV_DIM = ATT_KV_HEADS * ATT_HEAD_DIM
IDX_Q_DIM = IDX_HEADS * IDX_DIM
LANE = 128
SUBLANE = 8
INT_MIN = -(2 ** 31)
HALF_BIAS = 2 ** 15
NEG_BIG = -1e30
LOG2_E = 1.4426950408889634
SAFE_LOG2_LOGIT = 60.0
VMEM_LIMIT = 56 * 1024 * 1024

COL_ZSSM = 0
COL_GATE = COL_ZSSM + SSM_D_INNER
COL_ZATT = COL_GATE + 2 * D_MODEL
COL_Q = COL_ZATT + ATT_Q_DIM
COL_K = COL_Q + ATT_Q_DIM
COL_V = COL_K + ATT_KV_DIM
COL_QIDX = COL_V + ATT_KV_DIM
COL_KIDX = COL_QIDX + IDX_Q_DIM
WIDE_COLS = 7168
SMALL_DT = 0
SMALL_WIDX = SSM_HEADS


def _split3(x):
    hi = x.astype(jnp.bfloat16)
    r1 = x - hi.astype(F32)
    mid = r1.astype(jnp.bfloat16)
    lo = (r1 - mid.astype(F32)).astype(jnp.bfloat16)
    return hi, mid, lo


def _dot(a, b):
    return jnp.dot(a, b, preferred_element_type=F32)


def _dot_f32(a, b):
    a0, a1, a2 = _split3(a)
    b0, b1, b2 = _split3(b)
    return (_dot(a0, b0) + (_dot(a0, b1) + _dot(a1, b0))
            + (_dot(a0, b2) + _dot(a1, b1) + _dot(a2, b0)))


def _dot_x_sel(x, sel):
    m = x.shape[0]
    parts = jnp.concatenate(_split3(x), axis=0)
    r = _dot(parts, sel)
    return r[:m] + r[m:2 * m] + r[2 * m:]


def _dot_sel_x(sel, x):
    hi, mid, lo = _split3(x)
    return _dot(sel, hi) + _dot(sel, mid) + _dot(sel, lo)


def _sigmoid(x):
    return 1.0 / (1.0 + jnp.exp(-x))


def _silu(x):
    return x * _sigmoid(x)


def _softplus(x):
    return jnp.maximum(x, 0.0) + jnp.log(1.0 + jnp.exp(-jnp.abs(x)))


def _ada_kernel(c_ref, w_ref, b_ref, o_ref):
    o_ref[...] = _dot_f32(_silu(c_ref[...]), w_ref[...]) + b_ref[...]


def _ada(c, ada_w, ada_b):
    bsz = c.shape[0]
    return pl.pallas_call(
        _ada_kernel,
        out_shape=jax.ShapeDtypeStruct((bsz, 3 * D_MODEL), F32),
        grid=(3,),
        in_specs=[pl.BlockSpec((bsz, D_MODEL), lambda j: (0, 0)),
                  pl.BlockSpec((D_MODEL, D_MODEL), lambda j: (0, j)),
                  pl.BlockSpec((1, D_MODEL), lambda j: (0, j))],
        out_specs=pl.BlockSpec((bsz, D_MODEL), lambda j: (0, j)),
        compiler_params=pltpu.CompilerParams(dimension_semantics=("arbitrary",),
                                             vmem_limit_bytes=VMEM_LIMIT),
        name="ada",
    )(c, ada_w, ada_b.reshape(1, -1))


def _norm_kernel(x_ref, g_ref, scale_ref, shift_ref, wsm_ref, h_ref, small_ref):
    x = x_ref[...]
    ms = jnp.mean(x * x, axis=-1, keepdims=True)
    y = x * lax.rsqrt(ms + EPS) * g_ref[...]
    h = y * (1.0 + scale_ref[...]) + shift_ref[...]
    h_ref[...] = h.astype(h_ref.dtype)
    small_ref[...] = _dot_f32(h, wsm_ref[...])


def _norm(x, norm_g, scale, shift, w_small, tm):
    bsz, s, d = x.shape
    return pl.pallas_call(
        _norm_kernel,
        out_shape=(jax.ShapeDtypeStruct((bsz, s, d), ACT_DTYPE),
                   jax.ShapeDtypeStruct((bsz, s, LANE), F32)),
        grid=(bsz, s // tm),
        in_specs=[pl.BlockSpec((None, tm, d), lambda b, i: (b, i, 0)),
                  pl.BlockSpec((1, d), lambda b, i: (0, 0)),
                  pl.BlockSpec((None, 1, d), lambda b, i: (b, 0, 0)),
                  pl.BlockSpec((None, 1, d), lambda b, i: (b, 0, 0)),
                  pl.BlockSpec((d, LANE), lambda b, i: (0, 0))],
        out_specs=(pl.BlockSpec((None, tm, d), lambda b, i: (b, i, 0)),
                   pl.BlockSpec((None, tm, LANE), lambda b, i: (b, i, 0))),
        compiler_params=pltpu.CompilerParams(dimension_semantics=("parallel", "parallel"),
                                             vmem_limit_bytes=VMEM_LIMIT),
        name="norm",
    )(x, norm_g.reshape(1, d), scale[:, None, :], shift[:, None, :], w_small)


def _proj_kernel(h_ref, w_ref, o_ref):
    o_ref[...] = _dot(h_ref[...], w_ref[...]).astype(o_ref.dtype)


def _proj_conv_kernel(h_ref, w_ref, cw_ref, cb_ref, o_ref, tail_ref, *, tiles_per_seq):
    i, j = pl.program_id(0), pl.program_id(1)
    tm = h_ref.shape[0]

    @pl.when(i % tiles_per_seq == 0)
    def _():
        tail_ref[j] = jnp.zeros(tail_ref.shape[1:], F32)

    acc = _dot(h_ref[...], w_ref[...])
    tail = tail_ref[j]
    x1, x2, x3 = tail[SUBLANE - 1:SUBLANE], tail[SUBLANE - 2:SUBLANE - 1], tail[SUBLANE - 3:SUBLANE - 2]
    w = [cw_ref[k:k + 1, :] for k in range(SSM_CONV)]
    first_row = lax.broadcasted_iota(jnp.int32, (SUBLANE, 1), 0) == 0

    def delay(u, u_before):
        r = pltpu.roll(u, 1, 0)
        return jnp.concatenate([jnp.where(first_row, u_before, r[:SUBLANE]), r[SUBLANE:]], axis=0)

    u = delay(w[0] * acc, w[0] * x1)
    u = delay(w[1] * acc + u, w[1] * x1 + w[0] * x2)
    u = delay(w[2] * acc + u, w[2] * x1 + w[1] * x2 + w[0] * x3)
    o_ref[...] = _silu(w[3] * acc + u + cb_ref[...]).astype(o_ref.dtype)
    tail_ref[j] = acc[tm - SUBLANE:, :]


def _proj(h2d, w, tm, tn, conv=None):
    t, d = h2d.shape
    n = w.shape[1]
    specs = [pl.BlockSpec((tm, d), lambda i, j: (i, 0)), pl.BlockSpec((d, tn), lambda i, j: (0, j))]
    args, scratch, kern, name = [h2d, w], [], _proj_kernel, "proj"
    if conv is not None:
        conv_w, conv_b, seq = conv
        assert seq % tm == 0
        specs += [pl.BlockSpec((SSM_CONV, tn), lambda i, j: (0, j)), pl.BlockSpec((1, tn), lambda i, j: (0, j))]
        args += [conv_w, conv_b.reshape(1, -1)]
        scratch = [pltpu.VMEM((n // tn, SUBLANE, tn), F32)]
        kern, name = functools.partial(_proj_conv_kernel, tiles_per_seq=seq // tm), "proj_conv"
    return pl.pallas_call(
        kern,
        out_shape=jax.ShapeDtypeStruct((t, n), ACT_DTYPE),
        grid=(t // tm, n // tn),
        in_specs=specs,
        out_specs=pl.BlockSpec((tm, tn), lambda i, j: (i, j)),
        scratch_shapes=scratch,
        compiler_params=pltpu.CompilerParams(dimension_semantics=("arbitrary", "arbitrary"),
                                             vmem_limit_bytes=VMEM_LIMIT),
        name=name,
    )(*args)


def _rope(y, cos_t, sin_t, first_half, half):
    fwd = pltpu.roll(y, LANE - half, 1)
    bwd = pltpu.roll(y, half, 1)
    return y * cos_t + jnp.where(first_half, fwd, bwd) * sin_t


def _prep_kernel(q_ref, k_ref, qi_ref, ki_ref, pos_ref, inv_ref, exp_ref, pass_ref,
                 qg_ref, kg_ref, lg_ref, lb_ref, qo_ref, ko_ref, qio_ref, kio_ref):
    tm = q_ref.shape[0]
    lane = lax.broadcasted_iota(jnp.int32, (tm, LANE), 1)
    ang = pos_ref[...].astype(F32) * inv_ref[...]
    cos_v, sin_v = jnp.cos(ang), jnp.sin(ang)
    ca = _dot_x_sel(cos_v, exp_ref[0]) + pass_ref[0:1, :]
    sa = _dot_x_sel(sin_v, exp_ref[1])
    ci = _dot_x_sel(cos_v, exp_ref[2]) + pass_ref[1:2, :]
    si = _dot_x_sel(sin_v, exp_ref[3])
    att_first = lane < ATT_ROT_DIM // 2
    idx_first = (lane % IDX_DIM) < IDX_ROT_DIM // 2

    def norm_rope(x, g):
        ms = jnp.mean(x * x, axis=-1, keepdims=True)
        y = x * lax.rsqrt(ms + EPS) * g
        return _rope(y, ca, sa, att_first, ATT_ROT_DIM // 2)

    q_scale = ATT_HEAD_DIM ** -0.5 * LOG2_E
    for h in range(ATT_HEADS):
        sl = slice(h * LANE, (h + 1) * LANE)
        qo_ref[:, sl] = (norm_rope(q_ref[:, sl].astype(F32), qg_ref[...]) * q_scale).astype(qo_ref.dtype)
    for h in range(ATT_KV_HEADS):
        sl = slice(h * LANE, (h + 1) * LANE)
        ko_ref[:, sl] = norm_rope(k_ref[:, sl].astype(F32), kg_ref[...]).astype(ko_ref.dtype)
    for j in range(IDX_Q_DIM // LANE):
        sl = slice(j * LANE, (j + 1) * LANE)
        y = _rope(qi_ref[:, sl].astype(F32), ci, si, idx_first, IDX_ROT_DIM // 2)
        qio_ref[:, sl] = (y * IDX_DIM ** -0.5).astype(qio_ref.dtype)
    x = ki_ref[...].astype(F32)
    real = lane < IDX_DIM
    mu = jnp.sum(jnp.where(real, x, 0.0), axis=-1, keepdims=True) * (1.0 / IDX_DIM)
    dlt = jnp.where(real, x - mu, 0.0)
    var = jnp.sum(dlt * dlt, axis=-1, keepdims=True) * (1.0 / IDX_DIM)
    y = dlt * lax.rsqrt(var + EPS) * lg_ref[...] + lb_ref[...]
    kio_ref[...] = _rope(y, ci, si, idx_first, IDX_ROT_DIM // 2).astype(kio_ref.dtype)


def _rope_constants():
    def inv_freq(rot):
        return jnp.power(ROPE_THETA, -(jnp.arange(0, rot, 2, dtype=F32) / rot))

    half_a, half_i = ATT_ROT_DIM // 2, IDX_ROT_DIM // 2
    inv = jnp.zeros((LANE,), F32).at[:half_a].set(inv_freq(ATT_ROT_DIM)).at[half_a:half_a + half_i].set(
        inv_freq(IDX_ROT_DIM))
    src = jnp.arange(LANE)[:, None]
    dst = jnp.arange(LANE)[None, :]
    a_lo = (dst < half_a) & (src == dst)
    a_hi = (dst >= half_a) & (dst < 2 * half_a) & (src == dst - half_a)
    d64 = dst % IDX_DIM
    i_lo = (d64 < half_i) & (src == half_a + d64)
    i_hi = (d64 >= half_i) & (d64 < 2 * half_i) & (src == half_a + d64 - half_i)
    as_f = lambda m: m.astype(F32)
    spread = jnp.stack([as_f(a_lo) + as_f(a_hi), as_f(a_hi) - as_f(a_lo),
                        as_f(i_lo) + as_f(i_hi), as_f(i_hi) - as_f(i_lo)]).astype(jnp.bfloat16)
    lanes = jnp.arange(LANE)
    passthrough = jnp.stack([as_f(lanes >= 2 * half_a), as_f(lanes % IDX_DIM >= 2 * half_i)])
    return inv.reshape(1, LANE), spread, passthrough


def _prep(proj, positions, q_norm_g, k_norm_g, ln_g, ln_b, tm):
    t = proj.shape[0]
    inv, spread, passthrough = _rope_constants()
    row = lambda w, c0: pl.BlockSpec((tm, w), lambda i: (i, c0 // w))
    par = pl.BlockSpec((1, LANE), lambda i: (0, 0))
    pad = lambda v: jnp.pad(v, (0, LANE - v.shape[0])).reshape(1, LANE)
    out = lambda w: pl.BlockSpec((tm, w), lambda i: (i, 0))
    return pl.pallas_call(
        _prep_kernel,
        out_shape=(jax.ShapeDtypeStruct((t, ATT_Q_DIM), MXU_DTYPE),
                   jax.ShapeDtypeStruct((t, ATT_KV_DIM), MXU_DTYPE),
                   jax.ShapeDtypeStruct((t, IDX_Q_DIM), MXU_DTYPE),
                   jax.ShapeDtypeStruct((t, LANE), MXU_DTYPE)),
        grid=(t // tm,),
        in_specs=[row(ATT_Q_DIM, COL_Q), row(ATT_KV_DIM, COL_K), row(IDX_Q_DIM, COL_QIDX), row(LANE, COL_KIDX),
                  pl.BlockSpec((tm, 1), lambda i: (i, 0)), par,
                  pl.BlockSpec(spread.shape, lambda i: (0, 0, 0)), pl.BlockSpec(passthrough.shape, lambda i: (0, 0)),
                  par, par, par, par],
        out_specs=(out(ATT_Q_DIM), out(ATT_KV_DIM), out(IDX_Q_DIM), out(LANE)),
        compiler_params=pltpu.CompilerParams(dimension_semantics=("parallel",),
                                             vmem_limit_bytes=VMEM_LIMIT),
        name="prep",
    )(proj, proj, proj, proj, positions.reshape(t, 1), inv, spread, passthrough,
      q_norm_g.reshape(1, LANE), k_norm_g.reshape(1, LANE), pad(ln_g), pad(ln_b))


def _attn_kernel(q_ref, qiT_ref, smT_ref, kT_ref, v_ref, ki_ref, qg_ref, kg_ref, o_ref,
                 keys_ref, hi_ref, lo_ref, m_ref, acc_ref, qcat_ref, *, tq, topk):
    i = pl.program_id(1)
    nchunk = i + 1
    tk = tq
    w_idx = smT_ref[SMALL_WIDX:SMALL_WIDX + IDX_HEADS, :] * (IDX_HEADS ** -0.5)
    qi_all = jnp.concatenate([qiT_ref[h * IDX_DIM:(h + 1) * IDX_DIM, :] for h in range(IDX_HEADS)], axis=1)
    row_iota = lax.broadcasted_iota(jnp.int32, (tk, tq), 0)
    col_iota = lax.broadcasted_iota(jnp.int32, (tk, tq), 1)

    def score_chunk(c, diagonal):
        r0 = pl.multiple_of(c * tk, tk)
        kc = ki_ref[pl.ds(r0, tk), :][:, :IDX_DIM]
        lg = _dot(kc, qi_all)
        s = w_idx[0:1, :] * jnp.maximum(lg[:, :tq], 0.0)
        for h in range(1, IDX_HEADS):
            s = s + w_idx[h:h + 1, :] * jnp.maximum(lg[:, h * tq:(h + 1) * tq], 0.0)
        bits = lax.bitcast_convert_type(s + 0.0, jnp.int32)
        key = jnp.where(bits < 0, bits ^ jnp.int32(0x7FFFFFFF), bits)
        if diagonal:
            key = jnp.where(row_iota <= col_iota, key, jnp.int32(INT_MIN))
        keys_ref[pl.ds(r0, tk), :] = key
        hi_ref[pl.ds(r0, tk), :] = (key >> 16).astype(jnp.int16)
        lo_ref[pl.ds(r0, tk), :] = ((key & 0xFFFF) - HALF_BIAS).astype(jnp.int16)

    def score_body(c, carry):
        score_chunk(c, False)
        return carry

    lax.fori_loop(0, nchunk - 1, score_body, 0)
    score_chunk(nchunk - 1, True)

    pack = 2 * SUBLANE

    def count16(ref, pred):
        def body(c, acc):
            kk = ref[pl.ds(pl.multiple_of(c * tk, tk), tk), :]
            hit = jnp.where(pred(kk), jnp.bfloat16(1), jnp.bfloat16(0))
            parts = [hit[r * pack:(r + 1) * pack, :] for r in range(tk // pack)]
            while len(parts) > 1:
                parts = [parts[n] + parts[n + 1] for n in range(0, len(parts), 2)]
            return acc + parts[0].astype(F32)
        acc = lax.fori_loop(0, nchunk, body, jnp.zeros((pack, tq), F32))
        return acc.sum(axis=0, keepdims=True)

    def bisect16(ref, want):
        def bit_body(j, ans):
            trial = ans + lax.shift_left(jnp.int32(1), 15 - j)
            t16 = trial.astype(jnp.int16)
            cnt = count16(ref, lambda kk: kk >= t16)
            return jnp.where(cnt >= want, trial, ans)
        return lax.fori_loop(0, 16, bit_body, jnp.full((1, tq), -HALF_BIAS, jnp.int32))

    ans_hi = bisect16(hi_ref, float(topk))
    hi16 = ans_hi.astype(jnp.int16)
    want_lo = float(topk) - count16(hi_ref, lambda kk: kk > hi16)

    def restrict_body(c, carry):
        sl = pl.ds(pl.multiple_of(c * tk, tk), tk)
        lo_ref[sl, :] = jnp.where(hi_ref[sl, :] == hi16, lo_ref[sl, :], jnp.int16(-HALF_BIAS))
        return carry

    lax.fori_loop(0, nchunk, restrict_body, 0)
    ans_lo = bisect16(lo_ref, want_lo)
    ans = lax.shift_left(ans_hi, 16) | (ans_lo + HALF_BIAS)
    thr = jnp.maximum(ans, jnp.int32(INT_MIN + 1))
    lo16 = ans_lo.astype(jnp.int16)
    need_f = want_lo - count16(lo_ref, lambda kk: kk > lo16)

    for g in range(ATT_KV_HEADS):
        qcat_ref[g] = jnp.concatenate(
            [q_ref[:, h * LANE:(h + 1) * LANE] for h in range(g * ATT_GROUP, (g + 1) * ATT_GROUP)], axis=0)
    acc_ref[...] = jnp.zeros(acc_ref.shape, F32)
    ones = jnp.ones((tk, LANE), MXU_DTYPE)

    bound = (ATT_HEAD_DIM ** 0.5 * LOG2_E) * jnp.max(jnp.abs(qg_ref[...])) * jnp.max(jnp.abs(kg_ref[...]))
    bounded = bound <= SAFE_LOG2_LOGIT

    before = jnp.where(col_iota < row_iota, 1.0, 0.0).astype(jnp.bfloat16)

    def tie_bias(c, taken):
        kk = keys_ref[pl.ds(pl.multiple_of(c * tk, tk), tk), :]
        eq = kk == thr
        eq_b = jnp.where(eq, 1.0, 0.0).astype(jnp.bfloat16)
        rank = _dot(before, eq_b) + taken
        tie = jnp.where(eq, jnp.where(rank < need_f, 0.0, NEG_BIG), NEG_BIG)
        bias = jnp.where(kk > thr, 0.0, tie)
        return bias.T, taken + jnp.sum(eq_b.astype(F32), axis=0, keepdims=True)

    def masked_logits(c, g, bias_t):
        r0 = pl.multiple_of(c * tk, tk)
        s = _dot(qcat_ref[g], kT_ref[g * LANE:(g + 1) * LANE, pl.ds(r0, tk)])
        vx = jnp.concatenate([v_ref[pl.ds(r0, tk), g * LANE:(g + 1) * LANE], ones], axis=1)
        return [s[r * tq:(r + 1) * tq, :] + bias_t for r in range(ATT_GROUP)], vx

    def plain_softmax(c, bias_t):
        for g in range(ATT_KV_HEADS):
            s, vx = masked_logits(c, g, bias_t)
            p = jnp.concatenate([jnp.exp2(sr).astype(MXU_DTYPE) for sr in s], axis=0)
            acc_ref[g] += _dot(p, vx)

    def online_softmax(c, bias_t):
        for g in range(ATT_KV_HEADS):
            s, vx = masked_logits(c, g, bias_t)
            s = jnp.concatenate(s, axis=0)
            m_old = m_ref[g]
            m_new = jnp.maximum(m_old, jnp.max(s, axis=1, keepdims=True))
            p = jnp.exp2(s - m_new)
            acc_ref[g] = jnp.exp2(m_old - m_new) * acc_ref[g] + _dot(p.astype(MXU_DTYPE), vx)
            m_ref[g] = m_new

    def run(bias_fn, softmax_fn):
        def body(c, taken):
            bias_t, taken = bias_fn(c, taken)
            softmax_fn(c, bias_t)
            return taken
        lax.fori_loop(0, nchunk, body, jnp.zeros((1, tq), F32))

    @pl.when(bounded)
    def _():
        run(tie_bias, plain_softmax)

    @pl.when(jnp.logical_not(bounded))
    def _():
        m_ref[...] = jnp.full(m_ref.shape, NEG_BIG, F32)
        run(tie_bias, online_softmax)

    for h in range(ATT_HEADS):
        g, rs = h // ATT_GROUP, slice((h % ATT_GROUP) * tq, (h % ATT_GROUP + 1) * tq)
        o_ref[:, h * LANE:(h + 1) * LANE] = (acc_ref[g, rs, :LANE] / acc_ref[g, rs, LANE:]).astype(o_ref.dtype)


def _attn(q, qiT, smallT, kT, proj3, ki, q_norm_g, k_norm_g, tq):
    bsz, s, _ = q.shape
    topk = min(TOPK_MAX, s // 4)
    kern = functools.partial(_attn_kernel, tq=tq, topk=topk)
    qtile_t = lambda r: pl.BlockSpec((None, r, tq), lambda b, i: (b, 0, i))
    par = pl.BlockSpec((1, LANE), lambda b, i: (0, 0))
    return pl.pallas_call(
        kern,
        out_shape=jax.ShapeDtypeStruct((bsz, s, ATT_Q_DIM), ACT_DTYPE),
        grid=(bsz, s // tq),
        in_specs=[pl.BlockSpec((None, tq, ATT_Q_DIM), lambda b, i: (b, i, 0)),
                  qtile_t(IDX_Q_DIM), qtile_t(LANE),
                  pl.BlockSpec((None, ATT_KV_DIM, s), lambda b, i: (b, 0, 0)),
                  pl.BlockSpec((None, s, ATT_KV_DIM), lambda b, i: (b, 0, COL_V // ATT_KV_DIM)),
                  pl.BlockSpec((None, s, LANE), lambda b, i: (b, 0, 0)), par, par],
        out_specs=pl.BlockSpec((None, tq, ATT_Q_DIM), lambda b, i: (b, i, 0)),
        scratch_shapes=[pltpu.VMEM((s, tq), jnp.int32),
                        pltpu.VMEM((s, tq), jnp.int16),
                        pltpu.VMEM((s, tq), jnp.int16),
                        pltpu.VMEM((ATT_KV_HEADS, ATT_GROUP * tq, 1), F32),
                        pltpu.VMEM((ATT_KV_HEADS, ATT_GROUP * tq, 2 * LANE), F32),
                        pltpu.VMEM((ATT_KV_HEADS, ATT_GROUP * tq, ATT_HEAD_DIM), MXU_DTYPE)],
        compiler_params=pltpu.CompilerParams(dimension_semantics=("parallel", "arbitrary"),
                                             vmem_limit_bytes=VMEM_LIMIT),
        name="attn",
    )(q, qiT, smallT, kT, proj3, ki, q_norm_g.reshape(1, LANE), k_norm_g.reshape(1, LANE))


def _ssm_kernel(xbc_ref, sm_ref, smT_ref, dtb_ref, dtbT_ref, a_ref, aT_ref,
                dsk_ref, exp_ref, y_ref, st_ref, yacc_ref):
    ln = SSM_CHUNK
    c = pl.program_id(1)

    @pl.when(c == 0)
    def _():
        st_ref[...] = jnp.zeros(st_ref.shape, F32)

    xs = xbc_ref[:, :SSM_D_INNER].astype(F32)
    b_all = xbc_ref[:, SSM_D_INNER:SSM_D_INNER + SSM_GROUPS * SSM_STATE].astype(MXU_DTYPE)
    c_all = xbc_ref[:, SSM_D_INNER + SSM_GROUPS * SSM_STATE:].astype(MXU_DTYPE)

    dt = _softplus(sm_ref[...] + dtb_ref[...])
    adt = dt * a_ref[...]
    r_i = lax.broadcasted_iota(jnp.int32, (ln, ln), 0)
    c_i = lax.broadcasted_iota(jnp.int32, (ln, ln), 1)
    tril = r_i >= c_i
    incl = jnp.where(tril, 1.0, 0.0).astype(jnp.bfloat16)
    a_col = _dot_sel_x(incl, adt)
    dtT = _softplus(smT_ref[...] + dtbT_ref[...])
    incl_t = jnp.where(r_i <= c_i, 1.0, 0.0).astype(jnp.bfloat16)
    a_row = _dot_x_sel(dtT * aT_ref[...], incl_t)

    expand = exp_ref[...]
    wide = _dot_x_sel(jnp.concatenate([a_col, dt], axis=0), expand)
    a_wide, dt_wide = wide[:ln], wide[ln:]
    a_last = a_wide[ln - 1:ln, :]
    xc = xs * dt_wide
    xc_b = xc.astype(MXU_DTYPE)
    xd_b = (xc * jnp.exp(a_last - a_wide)).astype(MXU_DTYPE)
    e_wide = jnp.exp(a_wide)
    chunk_decay = e_wide[ln - 1:ln, :]

    lane = lax.broadcasted_iota(jnp.int32, (ln, LANE), 1)
    low = lane < SSM_HEAD_DIM
    gw = SSM_D_INNER // SSM_GROUPS
    for g in range(SSM_GROUPS):
        gs = slice(g * gw, (g + 1) * gw)
        b_g = b_all[:, g * SSM_STATE:(g + 1) * SSM_STATE]
        c_g = c_all[:, g * SSM_STATE:(g + 1) * SSM_STATE]
        cb = lax.dot_general(c_g, b_g, (((1,), (1,)), ((), ())), preferred_element_type=F32)
        y_off = _dot(c_g, st_ref[:, gs].astype(MXU_DTYPE)) * e_wide[:, gs]
        st_ref[:, gs] = chunk_decay[:, gs] * st_ref[:, gs] + lax.dot_general(
            b_g, xd_b[:, gs], (((0,), (0,)), ((), ())), preferred_element_type=F32)
        yacc_ref[:, gs] = y_off
        for jp in range(gw // LANE):
            j = g * (gw // LANE) + jp
            xp = xc_b[:, j * LANE:(j + 1) * LANE]
            yd = None
            for half, keep in ((0, low), (1, jnp.logical_not(low))):
                hd = 2 * j + half
                diff = a_col[:, hd:hd + 1] - a_row[hd:hd + 1, :]
                lmat = jnp.exp(jnp.where(tril, diff, -jnp.inf))
                mh = (cb * lmat).astype(MXU_DTYPE)
                part = _dot(mh, jnp.where(keep, xp, jnp.zeros_like(xp)))
                yd = part if yd is None else yd + part
            yacc_ref[:, j * LANE:(j + 1) * LANE] += yd

    y_ref[...] = (yacc_ref[...] + dsk_ref[...] * xs).astype(y_ref.dtype)


def _ssm(xbc3, small, smallT, dt_bias, a_log, d_skip):
    bsz, s, _ = xbc3.shape
    ln = SSM_CHUNK
    padl = lambda v: jnp.pad(v, (0, LANE - v.shape[0]))
    a = padl(-jnp.exp(a_log))
    dtb = padl(dt_bias)
    head_of = jnp.arange(SSM_D_INNER) // SSM_HEAD_DIM
    expand = (jnp.arange(LANE)[:, None] == head_of[None, :]).astype(jnp.bfloat16)
    full = lambda shp: pl.BlockSpec(shp, lambda b, c: (0,) * len(shp))
    return pl.pallas_call(
        _ssm_kernel,
        out_shape=jax.ShapeDtypeStruct((bsz, s, SSM_D_INNER), ACT_DTYPE),
        grid=(bsz, s // ln),
        in_specs=[pl.BlockSpec((None, ln, SSM_CONV_DIM), lambda b, c: (b, c, 0)),
                  pl.BlockSpec((None, ln, LANE), lambda b, c: (b, c, 0)),
                  pl.BlockSpec((None, LANE, ln), lambda b, c: (b, 0, c)),
                  full((1, LANE)), full((LANE, 1)), full((1, LANE)), full((LANE, 1)),
                  full((1, SSM_D_INNER)), full((LANE, SSM_D_INNER))],
        out_specs=pl.BlockSpec((None, ln, SSM_D_INNER), lambda b, c: (b, c, 0)),
        scratch_shapes=[pltpu.VMEM((SSM_STATE, SSM_D_INNER), F32),
                        pltpu.VMEM((ln, SSM_D_INNER), F32)],
        compiler_params=pltpu.CompilerParams(dimension_semantics=("parallel", "arbitrary"),
                                             vmem_limit_bytes=VMEM_LIMIT),
        name="ssm",
    )(xbc3, small, smallT,
      dtb.reshape(1, LANE), dtb.reshape(LANE, 1), a.reshape(1, LANE), a.reshape(LANE, 1),
      jnp.repeat(d_skip, SSM_HEAD_DIM).reshape(1, -1), expand)


def _final_kernel(o_ref, za_ref, y_ref, zs_ref, ng_ref, ga_ref, gs_ref, x_ref, gate_ref, wa_ref, ws_ref, wo_ref,
                  out_ref):
    o = (o_ref[...].astype(F32) * _silu(za_ref[...].astype(F32))).astype(MXU_DTYPE)
    y_att = _dot(o, wa_ref[...])
    y = y_ref[...].astype(F32) * _silu(zs_ref[...].astype(F32))
    gw = SSM_D_INNER // SSM_GROUPS
    y_n = []
    for g in range(SSM_GROUPS):
        yg = y[:, g * gw:(g + 1) * gw]
        ms = jnp.mean(yg * yg, axis=-1, keepdims=True)
        y_n.append((yg * lax.rsqrt(ms + EPS) * ng_ref[:, g * gw:(g + 1) * gw]).astype(MXU_DTYPE))
    y_ssm = _dot(jnp.concatenate(y_n, axis=1), ws_ref[...])
    merged = _sigmoid(ga_ref[...].astype(F32)) * y_att + _sigmoid(gs_ref[...].astype(F32)) * y_ssm
    out_ref[...] = x_ref[...] + gate_ref[...] * _dot(merged.astype(MXU_DTYPE), wo_ref[...])


def _final(o3, proj3, y3, x, gate, ssm_norm_g, w_att, w_ssm, w_out, tm):
    bsz, s, d = x.shape
    tile = lambda w, c0: pl.BlockSpec((None, tm, w), lambda b, i: (b, i, c0 // w))
    full = lambda shp: pl.BlockSpec(shp, lambda b, i: (0,) * len(shp))
    return pl.pallas_call(
        _final_kernel,
        out_shape=jax.ShapeDtypeStruct((bsz, s, d), x.dtype),
        grid=(bsz, s // tm),
        in_specs=[tile(ATT_Q_DIM, 0), tile(ATT_Q_DIM, COL_ZATT), tile(SSM_D_INNER, 0),
                  tile(SSM_D_INNER, COL_ZSSM), full((1, SSM_D_INNER)),
                  tile(D_MODEL, COL_GATE), tile(D_MODEL, COL_GATE + D_MODEL), tile(d, 0),
                  pl.BlockSpec((None, 1, d), lambda b, i: (b, 0, 0)),
                  full((ATT_Q_DIM, d)), full((SSM_D_INNER, d)), full((d, d))],
        out_specs=tile(d, 0),
        compiler_params=pltpu.CompilerParams(dimension_semantics=("parallel", "parallel"),
                                             vmem_limit_bytes=VMEM_LIMIT),
        name="final",
    )(o3, proj3, y3, proj3, ssm_norm_g.reshape(1, -1), proj3, proj3, x, gate[:, None, :],
      w_att.astype(MXU_DTYPE), w_ssm.astype(MXU_DTYPE), w_out.astype(MXU_DTYPE))


def _permute_w_in(w_in):
    sizes = (ATT_Q_DIM, ATT_KV_DIM, ATT_KV_DIM, ATT_Q_DIM, IDX_Q_DIM, IDX_DIM, IDX_HEADS,
             SSM_D_INNER, SSM_CONV_DIM, SSM_HEADS, 2 * D_MODEL)
    offs = [0]
    for sz in sizes:
        offs.append(offs[-1] + sz)
    cols = lambda w, n: w[:, offs[n]:offs[n + 1]]
    d = w_in.shape[0]
    w_cast = w_in.astype(MXU_DTYPE)
    q, k, v, z_att, q_idx, k_idx, _, z_ssm, xbc, _, gate = (cols(w_cast, n) for n in range(len(sizes)))
    wide = jnp.concatenate(
        [z_ssm, gate, z_att, q, k, v, q_idx, k_idx,
         jnp.zeros((d, WIDE_COLS - COL_KIDX - IDX_DIM), MXU_DTYPE)], axis=1)
    small = jnp.concatenate([cols(w_in, 9), cols(w_in, 6), jnp.zeros((d, LANE - SSM_HEADS - IDX_HEADS), w_in.dtype)],
                            axis=1)
    return xbc, wide, small


def _tile(n, pref):
    return pref if n % pref == 0 else n


def kernel(x, c, positions, ada_w, ada_b, norm_g, w_in, q_norm_g, k_norm_g, idx_k_ln_g, idx_k_ln_b,
           conv_w, conv_b, dt_bias, a_log, d_skip, ssm_norm_g, w_branch_att, w_branch_ssm, w_out):
    bsz, s, d = x.shape
    t = bsz * s
    for l in range(ada_w.shape[0]):
        mod = _ada(c, ada_w[l], ada_b[l])
        shift, scale, gate = mod[:, :d], mod[:, d:2 * d], mod[:, 2 * d:]
        w_xbc, w_wide, w_small = _permute_w_in(w_in[l])
        h, small = _norm(x, norm_g[l], scale, shift, w_small, _tile(s, 512))
        tm = _tile(s, 1024)
        xbc3 = _proj(h.reshape(t, d), w_xbc, tm, 1024, conv=(conv_w[l], conv_b[l], s)).reshape(bsz, s, -1)
        proj = _proj(h.reshape(t, d), w_wide, tm, 1024)
        proj3 = proj.reshape(bsz, s, WIDE_COLS)

        q, k, q_idx, k_idx = _prep(proj, positions, q_norm_g[l], k_norm_g[l], idx_k_ln_g[l], idx_k_ln_b[l],
                                   _tile(t, 512))
        to_t = lambda a: jnp.swapaxes(a.reshape(bsz, s, -1), 1, 2)
        smallT = jnp.swapaxes(small, 1, 2)
        o3 = _attn(q.reshape(bsz, s, -1), to_t(q_idx), smallT, to_t(k), proj3, k_idx.reshape(bsz, s, LANE),
                   q_norm_g[l], k_norm_g[l], _tile(s, 512))

        y3 = _ssm(xbc3, small, smallT, dt_bias[l], a_log[l], d_skip[l])
        x = _final(o3, proj3, y3, x, gate, ssm_norm_g[l], w_branch_att[l], w_branch_ssm[l], w_out[l], _tile(s, 512))
    return x
```

```python
import functools

import jax
import jax.numpy as jnp
from jax import lax
from jax.experimental import pallas as pl
from jax.experimental.pallas import tpu as pltpu

F32 = jnp.float32
MXU_DTYPE = jnp.bfloat16
ACT_DTYPE = jnp.bfloat16

D_MODEL = 1024
ATT_HEADS = 8
ATT_KV_HEADS = 2
ATT_HEAD_DIM = 128
ATT_GROUP = ATT_HEADS // ATT_KV_HEADS
ROPE_THETA = 500000.0
ATT_ROT_DIM = 32
IDX_HEADS = 4
IDX_DIM = 64
IDX_ROT_DIM = 16
TOPK_MAX = 256
SSM_D_INNER = 2048
SSM_HEAD_DIM = 64
SSM_HEADS = 32
SSM_GROUPS = 4
SSM_STATE = 128
SSM_CONV = 4
SSM_CHUNK = 128
SSM_CONV_DIM = SSM_D_INNER + 2 * SSM_GROUPS * SSM_STATE
EPS = 1e-6

ATT_Q_DIM = ATT_HEADS * ATT_HEAD_DIM
ATT_KV_DIM = ATT_KV_HEADS * ATT_HEAD_DIM
IDX_Q_DIM = IDX_HEADS * IDX_DIM
LANE = 128
SUBLANE = 8
MXU_DIM = 256
INT_MIN = -(2 ** 31)
HALF_BIAS = 2 ** 15
NEG_BIG = -1e30
LOG2_E = 1.4426950408889634
SAFE_LOG2_LOGIT = 60.0
VMEM_LIMIT = 56 * 1024 * 1024

COL_ZSSM = 0
COL_GATE = COL_ZSSM + SSM_D_INNER
COL_ZATT = COL_GATE + 2 * D_MODEL
COL_Q = COL_ZATT + ATT_Q_DIM
COL_K = COL_Q + ATT_Q_DIM
COL_V = COL_K + ATT_KV_DIM
COL_QIDX = COL_V + ATT_KV_DIM
COL_KIDX = COL_QIDX + IDX_Q_DIM
WIDE_COLS = 7168
SMALL_DT = 0
SMALL_WIDX = SSM_HEADS


def _split3(x):
    hi = x.astype(jnp.bfloat16)
    r1 = x - hi.astype(F32)
    mid = r1.astype(jnp.bfloat16)
    lo = (r1 - mid.astype(F32)).astype(jnp.bfloat16)
    return hi, mid, lo


def _dot(a, b):
    return jnp.dot(a, b, preferred_element_type=F32)


def _dot_f32(a, b):
    a0, a1, a2 = _split3(a)
    b0, b1, b2 = _split3(b)
    return (_dot(a0, b0) + (_dot(a0, b1) + _dot(a1, b0))
            + (_dot(a0, b2) + _dot(a1, b1) + _dot(a2, b0)))


def _dot_x_sel(x, sel):
    m = x.shape[0]
    parts = jnp.concatenate(_split3(x), axis=0)
    r = _dot(parts, sel)
    return r[:m] + r[m:2 * m] + r[2 * m:]


def _dot_sel_x(sel, x):
    hi, mid, lo = _split3(x)
    return _dot(sel, hi) + _dot(sel, mid) + _dot(sel, lo)


def _sigmoid(x):
    return 1.0 / (1.0 + jnp.exp(-x))


def _silu(x):
    return x * _sigmoid(x)


def _softplus(x):
    return jnp.maximum(x, 0.0) + jnp.log(1.0 + jnp.exp(-jnp.abs(x)))


def _ada_kernel(c_ref, w_ref, b_ref, o_ref):
    o_ref[...] = _dot_f32(_silu(c_ref[...]), w_ref[...]) + b_ref[...]


def _ada(c, ada_w, ada_b):
    bsz = c.shape[0]
    return pl.pallas_call(
        _ada_kernel,
        out_shape=jax.ShapeDtypeStruct((bsz, 3 * D_MODEL), F32),
        grid=(3,),
        in_specs=[pl.BlockSpec((bsz, D_MODEL), lambda j: (0, 0)),
                  pl.BlockSpec((D_MODEL, D_MODEL), lambda j: (0, j)),
                  pl.BlockSpec((1, D_MODEL), lambda j: (0, j))],
        out_specs=pl.BlockSpec((bsz, D_MODEL), lambda j: (0, j)),
        compiler_params=pltpu.CompilerParams(dimension_semantics=("arbitrary",),
                                             vmem_limit_bytes=VMEM_LIMIT),
        name="ada",
    )(c, ada_w, ada_b.reshape(1, -1))


def _norm_kernel(x_ref, g_ref, scale_ref, shift_ref, wsm_ref, h_ref, small_ref):
    x = x_ref[...]
    ms = jnp.mean(x * x, axis=-1, keepdims=True)
    y = x * lax.rsqrt(ms + EPS) * g_ref[...]
    h = y * (1.0 + scale_ref[...]) + shift_ref[...]
    h_ref[...] = h.astype(h_ref.dtype)
    small_ref[...] = _dot_f32(h, wsm_ref[...])


def _norm(x, norm_g, scale, shift, w_small, tm):
    bsz, s, d = x.shape
    return pl.pallas_call(
        _norm_kernel,
        out_shape=(jax.ShapeDtypeStruct((bsz, s, d), ACT_DTYPE),
                   jax.ShapeDtypeStruct((bsz, s, LANE), F32)),
        grid=(bsz, s // tm),
        in_specs=[pl.BlockSpec((None, tm, d), lambda b, i: (b, i, 0)),
                  pl.BlockSpec((1, d), lambda b, i: (0, 0)),
                  pl.BlockSpec((None, 1, d), lambda b, i: (b, 0, 0)),
                  pl.BlockSpec((None, 1, d), lambda b, i: (b, 0, 0)),
                  pl.BlockSpec((d, LANE), lambda b, i: (0, 0))],
        out_specs=(pl.BlockSpec((None, tm, d), lambda b, i: (b, i, 0)),
                   pl.BlockSpec((None, tm, LANE), lambda b, i: (b, i, 0))),
        compiler_params=pltpu.CompilerParams(dimension_semantics=("parallel", "parallel"),
                                             vmem_limit_bytes=VMEM_LIMIT),
        name="norm",
    )(x, norm_g.reshape(1, d), scale[:, None, :], shift[:, None, :], w_small)


def _proj_kernel(h_ref, w_ref, o_ref):
    o_ref[...] = _dot(h_ref[...], w_ref[...]).astype(o_ref.dtype)


def _proj_conv_kernel(h_ref, w_ref, cw_ref, cb_ref, o_ref, tail_ref, *, tiles_per_seq):
    i, j = pl.program_id(0), pl.program_id(1)
    tm = h_ref.shape[0]

    @pl.when(i % tiles_per_seq == 0)
    def _():
        tail_ref[j] = jnp.zeros(tail_ref.shape[1:], F32)

    acc = _dot(h_ref[...], w_ref[...])
    tail = tail_ref[j]
    x1, x2, x3 = tail[SUBLANE - 1:SUBLANE], tail[SUBLANE - 2:SUBLANE - 1], tail[SUBLANE - 3:SUBLANE - 2]
    w = [cw_ref[k:k + 1, :] for k in range(SSM_CONV)]
    first_row = lax.broadcasted_iota(jnp.int32, (SUBLANE, 1), 0) == 0

    def delay(u, u_before):
        r = pltpu.roll(u, 1, 0)
        return jnp.concatenate([jnp.where(first_row, u_before, r[:SUBLANE]), r[SUBLANE:]], axis=0)

    u = delay(w[0] * acc, w[0] * x1)
    u = delay(w[1] * acc + u, w[1] * x1 + w[0] * x2)
    u = delay(w[2] * acc + u, w[2] * x1 + w[1] * x2 + w[0] * x3)
    o_ref[...] = _silu(w[3] * acc + u + cb_ref[...]).astype(o_ref.dtype)
    tail_ref[j] = acc[tm - SUBLANE:, :]


def _proj(h2d, w, tm, tn, conv=None):
    t, d = h2d.shape
    n = w.shape[1]
    specs = [pl.BlockSpec((tm, d), lambda i, j: (i, 0)), pl.BlockSpec((d, tn), lambda i, j: (0, j))]
    args, scratch, kern, name = [h2d, w], [], _proj_kernel, "proj"
    if conv is not None:
        conv_w, conv_b, seq = conv
        assert seq % tm == 0
        specs += [pl.BlockSpec((SSM_CONV, tn), lambda i, j: (0, j)), pl.BlockSpec((1, tn), lambda i, j: (0, j))]
        args += [conv_w, conv_b.reshape(1, -1)]
        scratch = [pltpu.VMEM((n // tn, SUBLANE, tn), F32)]
        kern, name = functools.partial(_proj_conv_kernel, tiles_per_seq=seq // tm), "proj_conv"
    return pl.pallas_call(
        kern,
        out_shape=jax.ShapeDtypeStruct((t, n), ACT_DTYPE),
        grid=(t // tm, n // tn),
        in_specs=specs,
        out_specs=pl.BlockSpec((tm, tn), lambda i, j: (i, j)),
        scratch_shapes=scratch,
        compiler_params=pltpu.CompilerParams(dimension_semantics=("arbitrary", "arbitrary"),
                                             vmem_limit_bytes=VMEM_LIMIT),
        name=name,
    )(*args)


def _rope(y, cos_t, sin_t, first_half, half):
    fwd = pltpu.roll(y, LANE - half, 1)
    bwd = pltpu.roll(y, half, 1)
    return y * cos_t + jnp.where(first_half, fwd, bwd) * sin_t


def _prep_kernel(q_ref, k_ref, qi_ref, ki_ref, pos_ref, inv_ref, exp_ref, pass_ref,
                 qg_ref, kg_ref, lg_ref, lb_ref, qo_ref, ko_ref, qio_ref, kio_ref):
    tm = q_ref.shape[0]
    lane = lax.broadcasted_iota(jnp.int32, (tm, LANE), 1)
    ang = pos_ref[...].astype(F32) * inv_ref[...]
    cos_v, sin_v = jnp.cos(ang), jnp.sin(ang)
    ca = _dot_x_sel(cos_v, exp_ref[0]) + pass_ref[0:1, :]
    sa = _dot_x_sel(sin_v, exp_ref[1])
    ci = _dot_x_sel(cos_v, exp_ref[2]) + pass_ref[1:2, :]
    si = _dot_x_sel(sin_v, exp_ref[3])
    att_first = lane < ATT_ROT_DIM // 2
    idx_first = (lane % IDX_DIM) < IDX_ROT_DIM // 2

    def norm_rope(x, g):
        ms = jnp.mean(x * x, axis=-1, keepdims=True)
        y = x * lax.rsqrt(ms + EPS) * g
        return _rope(y, ca, sa, att_first, ATT_ROT_DIM // 2)

    q_scale = ATT_HEAD_DIM ** -0.5 * LOG2_E
    for h in range(ATT_HEADS):
        sl = slice(h * LANE, (h + 1) * LANE)
        qo_ref[:, sl] = (norm_rope(q_ref[:, sl].astype(F32), qg_ref[...]) * q_scale).astype(qo_ref.dtype)
    for h in range(ATT_KV_HEADS):
        sl = slice(h * LANE, (h + 1) * LANE)
        ko_ref[:, sl] = norm_rope(k_ref[:, sl].astype(F32), kg_ref[...]).astype(ko_ref.dtype)
    for j in range(IDX_Q_DIM // LANE):
        sl = slice(j * LANE, (j + 1) * LANE)
        y = _rope(qi_ref[:, sl].astype(F32), ci, si, idx_first, IDX_ROT_DIM // 2)
        qio_ref[:, sl] = (y * IDX_DIM ** -0.5).astype(qio_ref.dtype)
    x = ki_ref[...].astype(F32)
    real = lane < IDX_DIM
    mu = jnp.sum(jnp.where(real, x, 0.0), axis=-1, keepdims=True) * (1.0 / IDX_DIM)
    dlt = jnp.where(real, x - mu, 0.0)
    var = jnp.sum(dlt * dlt, axis=-1, keepdims=True) * (1.0 / IDX_DIM)
    y = dlt * lax.rsqrt(var + EPS) * lg_ref[...] + lb_ref[...]
    kio_ref[...] = _rope(y, ci, si, idx_first, IDX_ROT_DIM // 2).astype(kio_ref.dtype)


def _rope_constants():
    def inv_freq(rot):
        return jnp.power(ROPE_THETA, -(jnp.arange(0, rot, 2, dtype=F32) / rot))

    half_a, half_i = ATT_ROT_DIM // 2, IDX_ROT_DIM // 2
    inv = jnp.zeros((LANE,), F32).at[:half_a].set(inv_freq(ATT_ROT_DIM)).at[half_a:half_a + half_i].set(
        inv_freq(IDX_ROT_DIM))
    src = jnp.arange(LANE)[:, None]
    dst = jnp.arange(LANE)[None, :]
    a_lo = (dst < half_a) & (src == dst)
    a_hi = (dst >= half_a) & (dst < 2 * half_a) & (src == dst - half_a)
    d64 = dst % IDX_DIM
    i_lo = (d64 < half_i) & (src == half_a + d64)
    i_hi = (d64 >= half_i) & (d64 < 2 * half_i) & (src == half_a + d64 - half_i)
    as_f = lambda m: m.astype(F32)
    spread = jnp.stack([as_f(a_lo) + as_f(a_hi), as_f(a_hi) - as_f(a_lo),
                        as_f(i_lo) + as_f(i_hi), as_f(i_hi) - as_f(i_lo)]).astype(jnp.bfloat16)
    lanes = jnp.arange(LANE)
    passthrough = jnp.stack([as_f(lanes >= 2 * half_a), as_f(lanes % IDX_DIM >= 2 * half_i)])
    return inv.reshape(1, LANE), spread, passthrough


def _prep(proj, positions, q_norm_g, k_norm_g, ln_g, ln_b, tm):
    t = proj.shape[0]
    inv, spread, passthrough = _rope_constants()
    row = lambda w, c0: pl.BlockSpec((tm, w), lambda i: (i, c0 // w))
    par = pl.BlockSpec((1, LANE), lambda i: (0, 0))
    pad = lambda v: jnp.pad(v, (0, LANE - v.shape[0])).reshape(1, LANE)
    out = lambda w: pl.BlockSpec((tm, w), lambda i: (i, 0))
    return pl.pallas_call(
        _prep_kernel,
        out_shape=(jax.ShapeDtypeStruct((t, ATT_Q_DIM), MXU_DTYPE),
                   jax.ShapeDtypeStruct((t, ATT_KV_DIM), MXU_DTYPE),
                   jax.ShapeDtypeStruct((t, IDX_Q_DIM), MXU_DTYPE),
                   jax.ShapeDtypeStruct((t, LANE), MXU_DTYPE)),
        grid=(t // tm,),
        in_specs=[row(ATT_Q_DIM, COL_Q), row(ATT_KV_DIM, COL_K), row(IDX_Q_DIM, COL_QIDX), row(LANE, COL_KIDX),
                  pl.BlockSpec((tm, 1), lambda i: (i, 0)), par,
                  pl.BlockSpec(spread.shape, lambda i: (0, 0, 0)), pl.BlockSpec(passthrough.shape, lambda i: (0, 0)),
                  par, par, par, par],
        out_specs=(out(ATT_Q_DIM), out(ATT_KV_DIM), out(IDX_Q_DIM), out(LANE)),
        compiler_params=pltpu.CompilerParams(dimension_semantics=("parallel",),
                                             vmem_limit_bytes=VMEM_LIMIT),
        name="prep",
    )(proj, proj, proj, proj, positions.reshape(t, 1), inv, spread, passthrough,
      q_norm_g.reshape(1, LANE), k_norm_g.reshape(1, LANE), pad(ln_g), pad(ln_b))


def _attn_kernel(q_ref, qiT_ref, smT_ref, kT_ref, v_ref, ki_ref, qg_ref, kg_ref, o_ref,
                 keys_ref, hi_ref, lo_ref, m_ref, acc_ref, qcat_ref, *, tq, topk):
    i = pl.program_id(1)
    nchunk = i + 1
    tk = tq
    w_idx = smT_ref[SMALL_WIDX:SMALL_WIDX + IDX_HEADS, :] * (IDX_HEADS ** -0.5)
    qi_all = jnp.concatenate([qiT_ref[h * IDX_DIM:(h + 1) * IDX_DIM, :] for h in range(IDX_HEADS)], axis=1)
    row_iota = lax.broadcasted_iota(jnp.int32, (tk, tq), 0)
    col_iota = lax.broadcasted_iota(jnp.int32, (tk, tq), 1)

    def score_chunk(c, diagonal):
        r0 = pl.multiple_of(c * tk, tk)
        kc = ki_ref[pl.ds(r0, tk), :][:, :IDX_DIM]
        lg = _dot(kc, qi_all)
        s = w_idx[0:1, :] * jnp.maximum(lg[:, :tq], 0.0)
        for h in range(1, IDX_HEADS):
            s = s + w_idx[h:h + 1, :] * jnp.maximum(lg[:, h * tq:(h + 1) * tq], 0.0)
        bits = lax.bitcast_convert_type(s + 0.0, jnp.int32)
        key = jnp.where(bits < 0, bits ^ jnp.int32(0x7FFFFFFF), bits)
        if diagonal:
            key = jnp.where(row_iota <= col_iota, key, jnp.int32(INT_MIN))
        keys_ref[pl.ds(r0, tk), :] = key
        hi_ref[pl.ds(r0, tk), :] = (key >> 16).astype(jnp.int16)
        lo_ref[pl.ds(r0, tk), :] = ((key & 0xFFFF) - HALF_BIAS).astype(jnp.int16)

    def score_body(c, carry):
        score_chunk(c, False)
        return carry

    lax.fori_loop(0, nchunk - 1, score_body, 0)
    score_chunk(nchunk - 1, True)

    pack = 2 * SUBLANE

    def count16(ref, pred):
        def body(c, acc):
            kk = ref[pl.ds(pl.multiple_of(c * tk, tk), tk), :]
            hit = jnp.where(pred(kk), jnp.bfloat16(1), jnp.bfloat16(0))
            parts = [hit[r * pack:(r + 1) * pack, :] for r in range(tk // pack)]
            while len(parts) > 1:
                parts = [parts[n] + parts[n + 1] for n in range(0, len(parts), 2)]
            return acc + parts[0].astype(F32)
        acc = lax.fori_loop(0, nchunk, body, jnp.zeros((pack, tq), F32))
        return acc.sum(axis=0, keepdims=True)

    def bisect16(ref, want):
        def bit_body(j, ans):
            trial = ans + lax.shift_left(jnp.int32(1), 15 - j)
            t16 = trial.astype(jnp.int16)
            cnt = count16(ref, lambda kk: kk >= t16)
            return jnp.where(cnt >= want, trial, ans)
        return lax.fori_loop(0, 16, bit_body, jnp.full((1, tq), -HALF_BIAS, jnp.int32))

    ans_hi = bisect16(hi_ref, float(topk))
    hi16 = ans_hi.astype(jnp.int16)
    want_lo = float(topk) - count16(hi_ref, lambda kk: kk > hi16)

    def restrict_body(c, carry):
        sl = pl.ds(pl.multiple_of(c * tk, tk), tk)
        lo_ref[sl, :] = jnp.where(hi_ref[sl, :] == hi16, lo_ref[sl, :], jnp.int16(-HALF_BIAS))
        return carry

    lax.fori_loop(0, nchunk, restrict_body, 0)
    ans_lo = bisect16(lo_ref, want_lo)
    ans = lax.shift_left(ans_hi, 16) | (ans_lo + HALF_BIAS)
    thr = jnp.maximum(ans, jnp.int32(INT_MIN + 1))
    lo16 = ans_lo.astype(jnp.int16)
    need_f = want_lo - count16(lo_ref, lambda kk: kk > lo16)

    for g in range(ATT_KV_HEADS):
        qcat_ref[g] = jnp.concatenate(
            [q_ref[:, h * LANE:(h + 1) * LANE] for h in range(g * ATT_GROUP, (g + 1) * ATT_GROUP)], axis=0)
    acc_ref[...] = jnp.zeros(acc_ref.shape, F32)
    ones = jnp.ones((tk, LANE), MXU_DTYPE)

    bound = (ATT_HEAD_DIM ** 0.5 * LOG2_E) * jnp.max(jnp.abs(qg_ref[...])) * jnp.max(jnp.abs(kg_ref[...]))
    bounded = bound <= SAFE_LOG2_LOGIT

    rb = min(tk, MXU_DIM)
    before = jnp.where(lax.broadcasted_iota(jnp.int32, (rb, rb), 1) < lax.broadcasted_iota(jnp.int32, (rb, rb), 0),
                       1.0, 0.0).astype(jnp.bfloat16)

    def tie_bias(c, taken):
        kk = keys_ref[pl.ds(pl.multiple_of(c * tk, tk), tk), :]
        eq = kk == thr
        eq_b = jnp.where(eq, 1.0, 0.0).astype(jnp.bfloat16)
        ranks = []
        for r in range(tk // rb):
            blk = eq_b[r * rb:(r + 1) * rb, :]
            ranks.append(_dot(before, blk) + taken)
            taken = taken + jnp.sum(blk.astype(F32), axis=0, keepdims=True)
        rank = jnp.concatenate(ranks, axis=0)
        tie = jnp.where(eq, jnp.where(rank < need_f, 0.0, NEG_BIG), NEG_BIG)
        bias = jnp.where(kk > thr, 0.0, tie)
        return bias.T, taken

    def masked_logits(c, g, bias_t):
        r0 = pl.multiple_of(c * tk, tk)
        s = _dot(qcat_ref[g], kT_ref[g * LANE:(g + 1) * LANE, pl.ds(r0, tk)])
        vx = jnp.concatenate([v_ref[pl.ds(r0, tk), g * LANE:(g + 1) * LANE], ones], axis=1)
        return [s[r * tq:(r + 1) * tq, :] + bias_t for r in range(ATT_GROUP)], vx

    def plain_softmax(c, bias_t):
        for g in range(ATT_KV_HEADS):
            s, vx = masked_logits(c, g, bias_t)
            p = jnp.concatenate([jnp.exp2(sr).astype(MXU_DTYPE) for sr in s], axis=0)
            acc_ref[g] += _dot(p, vx)

    def online_softmax(c, bias_t):
        for g in range(ATT_KV_HEADS):
            s, vx = masked_logits(c, g, bias_t)
            s = jnp.concatenate(s, axis=0)
            m_old = m_ref[g]
            m_new = jnp.maximum(m_old, jnp.max(s, axis=1, keepdims=True))
            p = jnp.exp2(s - m_new)
            acc_ref[g] = jnp.exp2(m_old - m_new) * acc_ref[g] + _dot(p.astype(MXU_DTYPE), vx)
            m_ref[g] = m_new

    def run(bias_fn, softmax_fn):
        def body(c, taken):
            bias_t, taken = bias_fn(c, taken)
            softmax_fn(c, bias_t)
            return taken
        lax.fori_loop(0, nchunk, body, jnp.zeros((1, tq), F32))

    @pl.when(bounded)
    def _():
        run(tie_bias, plain_softmax)

    @pl.when(jnp.logical_not(bounded))
    def _():
        m_ref[...] = jnp.full(m_ref.shape, NEG_BIG, F32)
        run(tie_bias, online_softmax)

    for h in range(ATT_HEADS):
        g, rs = h // ATT_GROUP, slice((h % ATT_GROUP) * tq, (h % ATT_GROUP + 1) * tq)
        o_ref[:, h * LANE:(h + 1) * LANE] = (acc_ref[g, rs, :LANE] / acc_ref[g, rs, LANE:]).astype(o_ref.dtype)


def _attn(q, qiT, smallT, kT, proj3, ki, q_norm_g, k_norm_g, tq):
    bsz, s, _ = q.shape
    topk = min(TOPK_MAX, s // 4)
    kern = functools.partial(_attn_kernel, tq=tq, topk=topk)
    qtile_t = lambda r: pl.BlockSpec((None, r, tq), lambda b, i: (b, 0, i))
    par = pl.BlockSpec((1, LANE), lambda b, i: (0, 0))
    return pl.pallas_call(
        kern,
        out_shape=jax.ShapeDtypeStruct((bsz, s, ATT_Q_DIM), ACT_DTYPE),
        grid=(bsz, s // tq),
        in_specs=[pl.BlockSpec((None, tq, ATT_Q_DIM), lambda b, i: (b, i, 0)),
                  qtile_t(IDX_Q_DIM), qtile_t(LANE),
                  pl.BlockSpec((None, ATT_KV_DIM, s), lambda b, i: (b, 0, 0)),
                  pl.BlockSpec((None, s, ATT_KV_DIM), lambda b, i: (b, 0, COL_V // ATT_KV_DIM)),
                  pl.BlockSpec((None, s, LANE), lambda b, i: (b, 0, 0)), par, par],
        out_specs=pl.BlockSpec((None, tq, ATT_Q_DIM), lambda b, i: (b, i, 0)),
        scratch_shapes=[pltpu.VMEM((s, tq), jnp.int32),
                        pltpu.VMEM((s, tq), jnp.int16),
                        pltpu.VMEM((s, tq), jnp.int16),
                        pltpu.VMEM((ATT_KV_HEADS, ATT_GROUP * tq, 1), F32),
                        pltpu.VMEM((ATT_KV_HEADS, ATT_GROUP * tq, 2 * LANE), F32),
                        pltpu.VMEM((ATT_KV_HEADS, ATT_GROUP * tq, ATT_HEAD_DIM), MXU_DTYPE)],
        compiler_params=pltpu.CompilerParams(dimension_semantics=("parallel", "arbitrary"),
                                             vmem_limit_bytes=VMEM_LIMIT),
        name="attn",
    )(q, qiT, smallT, kT, proj3, ki, q_norm_g.reshape(1, LANE), k_norm_g.reshape(1, LANE))


def _ssm_kernel(xbc_ref, sm_ref, smT_ref, dtb_ref, dtbT_ref, a_ref, aT_ref,
                dsk_ref, exp_ref, y_ref, st_ref, yacc_ref):
    ln = SSM_CHUNK
    c = pl.program_id(1)

    @pl.when(c == 0)
    def _():
        st_ref[...] = jnp.zeros(st_ref.shape, F32)

    xs = xbc_ref[:, :SSM_D_INNER].astype(F32)
    b_all = xbc_ref[:, SSM_D_INNER:SSM_D_INNER + SSM_GROUPS * SSM_STATE].astype(MXU_DTYPE)
    c_all = xbc_ref[:, SSM_D_INNER + SSM_GROUPS * SSM_STATE:].astype(MXU_DTYPE)

    dt = _softplus(sm_ref[...] + dtb_ref[...])
    adt = dt * a_ref[...]
    r_i = lax.broadcasted_iota(jnp.int32, (ln, ln), 0)
    c_i = lax.broadcasted_iota(jnp.int32, (ln, ln), 1)
    tril = r_i >= c_i
    incl = jnp.where(tril, 1.0, 0.0).astype(jnp.bfloat16)
    a_col = _dot_sel_x(incl, adt)
    dtT = _softplus(smT_ref[...] + dtbT_ref[...])
    incl_t = jnp.where(r_i <= c_i, 1.0, 0.0).astype(jnp.bfloat16)
    a_row = _dot_x_sel(dtT * aT_ref[...], incl_t)

    expand = exp_ref[...]
    wide = _dot_x_sel(jnp.concatenate([a_col, dt], axis=0), expand)
    a_wide, dt_wide = wide[:ln], wide[ln:]
    a_last = a_wide[ln - 1:ln, :]
    xc = xs * dt_wide
    xc_b = xc.astype(MXU_DTYPE)
    xd_b = (xc * jnp.exp(a_last - a_wide)).astype(MXU_DTYPE)
    e_wide = jnp.exp(a_wide)
    chunk_decay = e_wide[ln - 1:ln, :]

    lane = lax.broadcasted_iota(jnp.int32, (ln, LANE), 1)
    low = lane < SSM_HEAD_DIM
    gw = SSM_D_INNER // SSM_GROUPS
    for g in range(SSM_GROUPS):
        gs = slice(g * gw, (g + 1) * gw)
        b_g = b_all[:, g * SSM_STATE:(g + 1) * SSM_STATE]
        c_g = c_all[:, g * SSM_STATE:(g + 1) * SSM_STATE]
        cb = lax.dot_general(c_g, b_g, (((1,), (1,)), ((), ())), preferred_element_type=F32)
        y_off = _dot(c_g, st_ref[:, gs].astype(MXU_DTYPE)) * e_wide[:, gs]
        st_ref[:, gs] = chunk_decay[:, gs] * st_ref[:, gs] + lax.dot_general(
            b_g, xd_b[:, gs], (((0,), (0,)), ((), ())), preferred_element_type=F32)
        yacc_ref[:, gs] = y_off
        for jp in range(gw // LANE):
            j = g * (gw // LANE) + jp
            xp = xc_b[:, j * LANE:(j + 1) * LANE]
            yd = None
            for half, keep in ((0, low), (1, jnp.logical_not(low))):
                hd = 2 * j + half
                diff = a_col[:, hd:hd + 1] - a_row[hd:hd + 1, :]
                lmat = jnp.exp(jnp.where(tril, diff, -jnp.inf))
                mh = (cb * lmat).astype(MXU_DTYPE)
                part = _dot(mh, jnp.where(keep, xp, jnp.zeros_like(xp)))
                yd = part if yd is None else yd + part
            yacc_ref[:, j * LANE:(j + 1) * LANE] += yd

    y_ref[...] = (yacc_ref[...] + dsk_ref[...] * xs).astype(y_ref.dtype)


def _ssm(xbc3, small, smallT, dt_bias, a_log, d_skip):
    bsz, s, _ = xbc3.shape
    ln = SSM_CHUNK
    padl = lambda v: jnp.pad(v, (0, LANE - v.shape[0]))
    a = padl(-jnp.exp(a_log))
    dtb = padl(dt_bias)
    head_of = jnp.arange(SSM_D_INNER) // SSM_HEAD_DIM
    expand = (jnp.arange(LANE)[:, None] == head_of[None, :]).astype(jnp.bfloat16)
    full = lambda shp: pl.BlockSpec(shp, lambda b, c: (0,) * len(shp))
    return pl.pallas_call(
        _ssm_kernel,
        out_shape=jax.ShapeDtypeStruct((bsz, s, SSM_D_INNER), ACT_DTYPE),
        grid=(bsz, s // ln),
        in_specs=[pl.BlockSpec((None, ln, SSM_CONV_DIM), lambda b, c: (b, c, 0)),
                  pl.BlockSpec((None, ln, LANE), lambda b, c: (b, c, 0)),
                  pl.BlockSpec((None, LANE, ln), lambda b, c: (b, 0, c)),
                  full((1, LANE)), full((LANE, 1)), full((1, LANE)), full((LANE, 1)),
                  full((1, SSM_D_INNER)), full((LANE, SSM_D_INNER))],
        out_specs=pl.BlockSpec((None, ln, SSM_D_INNER), lambda b, c: (b, c, 0)),
        scratch_shapes=[pltpu.VMEM((SSM_STATE, SSM_D_INNER), F32),
                        pltpu.VMEM((ln, SSM_D_INNER), F32)],
        compiler_params=pltpu.CompilerParams(dimension_semantics=("parallel", "arbitrary"),
                                             vmem_limit_bytes=VMEM_LIMIT),
        name="ssm",
    )(xbc3, small, smallT,
      dtb.reshape(1, LANE), dtb.reshape(LANE, 1), a.reshape(1, LANE), a.reshape(LANE, 1),
      jnp.repeat(d_skip, SSM_HEAD_DIM).reshape(1, -1), expand)


def _final_kernel(o_ref, za_ref, y_ref, zs_ref, ng_ref, ga_ref, gs_ref, x_ref, gate_ref, wa_ref, ws_ref, wo_ref,
                  out_ref):
    o = (o_ref[...].astype(F32) * _silu(za_ref[...].astype(F32))).astype(MXU_DTYPE)
    y_att = _dot(o, wa_ref[...])
    y = y_ref[...].astype(F32) * _silu(zs_ref[...].astype(F32))
    gw = SSM_D_INNER // SSM_GROUPS
    y_n = []
    for g in range(SSM_GROUPS):
        yg = y[:, g * gw:(g + 1) * gw]
        ms = jnp.mean(yg * yg, axis=-1, keepdims=True)
        y_n.append((yg * lax.rsqrt(ms + EPS) * ng_ref[:, g * gw:(g + 1) * gw]).astype(MXU_DTYPE))
    y_ssm = _dot(jnp.concatenate(y_n, axis=1), ws_ref[...])
    merged = _sigmoid(ga_ref[...].astype(F32)) * y_att + _sigmoid(gs_ref[...].astype(F32)) * y_ssm
    out_ref[...] = x_ref[...] + gate_ref[...] * _dot(merged.astype(MXU_DTYPE), wo_ref[...])


def _final(o3, proj3, y3, x, gate, ssm_norm_g, w_att, w_ssm, w_out, tm):
    bsz, s, d = x.shape
    tile = lambda w, c0: pl.BlockSpec((None, tm, w), lambda b, i: (b, i, c0 // w))
    full = lambda shp: pl.BlockSpec(shp, lambda b, i: (0,) * len(shp))
    return pl.pallas_call(
        _final_kernel,
        out_shape=jax.ShapeDtypeStruct((bsz, s, d), x.dtype),
        grid=(bsz, s // tm),
        in_specs=[tile(ATT_Q_DIM, 0), tile(ATT_Q_DIM, COL_ZATT), tile(SSM_D_INNER, 0),
                  tile(SSM_D_INNER, COL_ZSSM), full((1, SSM_D_INNER)),
                  tile(D_MODEL, COL_GATE), tile(D_MODEL, COL_GATE + D_MODEL), tile(d, 0),
                  pl.BlockSpec((None, 1, d), lambda b, i: (b, 0, 0)),
                  full((ATT_Q_DIM, d)), full((SSM_D_INNER, d)), full((d, d))],
        out_specs=tile(d, 0),
        compiler_params=pltpu.CompilerParams(dimension_semantics=("parallel", "parallel"),
                                             vmem_limit_bytes=VMEM_LIMIT),
        name="final",
    )(o3, proj3, y3, proj3, ssm_norm_g.reshape(1, -1), proj3, proj3, x, gate[:, None, :],
      w_att.astype(MXU_DTYPE), w_ssm.astype(MXU_DTYPE), w_out.astype(MXU_DTYPE))


def _permute_w_in(w_in):
    sizes = (ATT_Q_DIM, ATT_KV_DIM, ATT_KV_DIM, ATT_Q_DIM, IDX_Q_DIM, IDX_DIM, IDX_HEADS,
             SSM_D_INNER, SSM_CONV_DIM, SSM_HEADS, 2 * D_MODEL)
    offs = [0]
    for sz in sizes:
        offs.append(offs[-1] + sz)
    cols = lambda w, n: w[:, offs[n]:offs[n + 1]]
    d = w_in.shape[0]
    w_cast = w_in.astype(MXU_DTYPE)
    q, k, v, z_att, q_idx, k_idx, _, z_ssm, xbc, _, gate = (cols(w_cast, n) for n in range(len(sizes)))
    wide = jnp.concatenate(
        [z_ssm, gate, z_att, q, k, v, q_idx, k_idx,
         jnp.zeros((d, WIDE_COLS - COL_KIDX - IDX_DIM), MXU_DTYPE)], axis=1)
    small = jnp.concatenate([cols(w_in, 9), cols(w_in, 6), jnp.zeros((d, LANE - SSM_HEADS - IDX_HEADS), w_in.dtype)],
                            axis=1)
    return xbc, wide, small


def _tile(n, pref):
    return pref if n % pref == 0 else n


def kernel(x, c, positions, ada_w, ada_b, norm_g, w_in, q_norm_g, k_norm_g, idx_k_ln_g, idx_k_ln_b,
           conv_w, conv_b, dt_bias, a_log, d_skip, ssm_norm_g, w_branch_att, w_branch_ssm, w_out):
    bsz, s, d = x.shape
    t = bsz * s
    for l in range(ada_w.shape[0]):
        mod = _ada(c, ada_w[l], ada_b[l])
        shift, scale, gate = mod[:, :d], mod[:, d:2 * d], mod[:, 2 * d:]
        w_xbc, w_wide, w_small = _permute_w_in(w_in[l])
        h, small = _norm(x, norm_g[l], scale, shift, w_small, _tile(s, 512))
        tm = _tile(s, 1024)
        xbc3 = _proj(h.reshape(t, d), w_xbc, tm, 1024, conv=(conv_w[l], conv_b[l], s)).reshape(bsz, s, -1)
        proj = _proj(h.reshape(t, d), w_wide, tm, WIDE_COLS // 4)
        proj3 = proj.reshape(bsz, s, WIDE_COLS)

        q, k, q_idx, k_idx = _prep(proj, positions, q_norm_g[l], k_norm_g[l], idx_k_ln_g[l], idx_k_ln_b[l],
                                   _tile(t, 512))
        to_t = lambda a: jnp.swapaxes(a.reshape(bsz, s, -1), 1, 2)
        smallT = jnp.swapaxes(small, 1, 2)
        o3 = _attn(q.reshape(bsz, s, -1), to_t(q_idx), smallT, to_t(k), proj3, k_idx.reshape(bsz, s, LANE),
                   q_norm_g[l], k_norm_g[l], _tile(s, 512))

        y3 = _ssm(xbc3, small, smallT, dt_bias[l], a_log[l], d_skip[l])
        x = _final(o3, proj3, y3, x, gate, ssm_norm_g[l], w_branch_att[l], w_branch_ssm[l], w_out[l], _tile(s, 512))
    return x
```

```python
import functools

import jax
import jax.numpy as jnp
from jax import lax
from jax.experimental import pallas as pl
from jax.experimental.pallas import tpu as pltpu

F32 = jnp.float32
MXU_DTYPE = jnp.bfloat16
ACT_DTYPE = jnp.bfloat16

D_MODEL = 1024
ATT_HEADS = 8
ATT_KV_HEADS = 2
ATT_HEAD_DIM = 128
ATT_GROUP = ATT_HEADS // ATT_KV_HEADS
ROPE_THETA = 500000.0
ATT_ROT_DIM = 32
IDX_HEADS = 4
IDX_DIM = 64
IDX_ROT_DIM = 16
TOPK_MAX = 256
SSM_D_INNER = 2048
SSM_HEAD_DIM = 64
SSM_HEADS = 32
SSM_GROUPS = 4
SSM_STATE = 128
SSM_CONV = 4
SSM_CHUNK = 128
SSM_CONV_DIM = SSM_D_INNER + 2 * SSM_GROUPS * SSM_STATE
EPS = 1e-6

ATT_Q_DIM = ATT_HEADS * ATT_HEAD_DIM
ATT_KV_DIM = ATT_KV_HEADS * ATT_HEAD_DIM
IDX_Q_DIM = IDX_HEADS * IDX_DIM
LANE = 128
SUBLANE = 8
MXU_DIM = 256
INT_MIN = -(2 ** 31)
HALF_BIAS = 2 ** 15
NEG_BIG = -1e30
LOG2_E = 1.4426950408889634
SAFE_LOG2_LOGIT = 60.0
VMEM_LIMIT = 56 * 1024 * 1024

COL_ZSSM = 0
COL_GATE = COL_ZSSM + SSM_D_INNER
COL_ZATT = COL_GATE + 2 * D_MODEL
COL_Q = COL_ZATT + ATT_Q_DIM
COL_K = COL_Q + ATT_Q_DIM
COL_V = COL_K + ATT_KV_DIM
COL_QIDX = COL_V + ATT_KV_DIM
COL_KIDX = COL_QIDX + IDX_Q_DIM
WIDE_COLS = 7168
SMALL_DT = 0
SMALL_WIDX = SSM_HEADS


def _split3(x):
    hi = x.astype(jnp.bfloat16)
    r1 = x - hi.astype(F32)
    mid = r1.astype(jnp.bfloat16)
    lo = (r1 - mid.astype(F32)).astype(jnp.bfloat16)
    return hi, mid, lo


def _dot(a, b):
    return jnp.dot(a, b, preferred_element_type=F32)


def _dot_f32(a, b):
    a0, a1, a2 = _split3(a)
    b0, b1, b2 = _split3(b)
    return (_dot(a0, b0) + (_dot(a0, b1) + _dot(a1, b0))
            + (_dot(a0, b2) + _dot(a1, b1) + _dot(a2, b0)))


def _dot_x_sel(x, sel, pieces=3):
    m = x.shape[0]
    r = _dot(jnp.concatenate(_split3(x)[:pieces], axis=0), sel)
    return sum(r[n * m:(n + 1) * m] for n in range(1, pieces)) + r[:m]


def _dot_sel_x(sel, x):
    hi, mid, lo = _split3(x)
    return _dot(sel, hi) + _dot(sel, mid) + _dot(sel, lo)


def _sigmoid(x):
    return 1.0 / (1.0 + jnp.exp(-x))


def _silu(x):
    return x * _sigmoid(x)


def _softplus(x):
    return jnp.maximum(x, 0.0) + jnp.log(1.0 + jnp.exp(-jnp.abs(x)))


def _ada_kernel(c_ref, w_ref, b_ref, o_ref):
    o_ref[...] = _dot_f32(_silu(c_ref[...]), w_ref[...]) + b_ref[...]


def _ada(c, ada_w, ada_b):
    bsz = c.shape[0]
    return pl.pallas_call(
        _ada_kernel,
        out_shape=jax.ShapeDtypeStruct((bsz, 3 * D_MODEL), F32),
        grid=(3,),
        in_specs=[pl.BlockSpec((bsz, D_MODEL), lambda j: (0, 0)),
                  pl.BlockSpec((D_MODEL, D_MODEL), lambda j: (0, j)),
                  pl.BlockSpec((1, D_MODEL), lambda j: (0, j))],
        out_specs=pl.BlockSpec((bsz, D_MODEL), lambda j: (0, j)),
        compiler_params=pltpu.CompilerParams(dimension_semantics=("arbitrary",),
                                             vmem_limit_bytes=VMEM_LIMIT),
        name="ada",
    )(c, ada_w, ada_b.reshape(1, -1))


def _norm_kernel(x_ref, g_ref, scale_ref, shift_ref, wsm_ref, h_ref, small_ref):
    x = x_ref[...]
    ms = jnp.mean(x * x, axis=-1, keepdims=True)
    y = x * lax.rsqrt(ms + EPS) * g_ref[...]
    h = y * (1.0 + scale_ref[...]) + shift_ref[...]
    h_ref[...] = h.astype(h_ref.dtype)
    small_ref[...] = _dot_f32(h, wsm_ref[...])


def _norm(x, norm_g, scale, shift, w_small, tm):
    bsz, s, d = x.shape
    return pl.pallas_call(
        _norm_kernel,
        out_shape=(jax.ShapeDtypeStruct((bsz, s, d), ACT_DTYPE),
                   jax.ShapeDtypeStruct((bsz, s, LANE), F32)),
        grid=(bsz, s // tm),
        in_specs=[pl.BlockSpec((None, tm, d), lambda b, i: (b, i, 0)),
                  pl.BlockSpec((1, d), lambda b, i: (0, 0)),
                  pl.BlockSpec((None, 1, d), lambda b, i: (b, 0, 0)),
                  pl.BlockSpec((None, 1, d), lambda b, i: (b, 0, 0)),
                  pl.BlockSpec((d, LANE), lambda b, i: (0, 0))],
        out_specs=(pl.BlockSpec((None, tm, d), lambda b, i: (b, i, 0)),
                   pl.BlockSpec((None, tm, LANE), lambda b, i: (b, i, 0))),
        compiler_params=pltpu.CompilerParams(dimension_semantics=("parallel", "parallel"),
                                             vmem_limit_bytes=VMEM_LIMIT),
        name="norm",
    )(x, norm_g.reshape(1, d), scale[:, None, :], shift[:, None, :], w_small)


def _proj_kernel(h_ref, w_ref, o_ref):
    o_ref[...] = _dot(h_ref[...], w_ref[...]).astype(o_ref.dtype)


def _proj_conv_kernel(h_ref, w_ref, cw_ref, cb_ref, o_ref, tail_ref, *, tiles_per_seq):
    i, j = pl.program_id(0), pl.program_id(1)
    tm = h_ref.shape[0]

    @pl.when(i % tiles_per_seq == 0)
    def _():
        tail_ref[j] = jnp.zeros(tail_ref.shape[1:], F32)

    acc = _dot(h_ref[...], w_ref[...])
    tail = tail_ref[j]
    x1, x2, x3 = tail[SUBLANE - 1:SUBLANE], tail[SUBLANE - 2:SUBLANE - 1], tail[SUBLANE - 3:SUBLANE - 2]
    w = [cw_ref[k:k + 1, :] for k in range(SSM_CONV)]
    first_row = lax.broadcasted_iota(jnp.int32, (SUBLANE, 1), 0) == 0

    def delay(u, u_before):
        r = pltpu.roll(u, 1, 0)
        return jnp.concatenate([jnp.where(first_row, u_before, r[:SUBLANE]), r[SUBLANE:]], axis=0)

    u = delay(w[0] * acc, w[0] * x1)
    u = delay(w[1] * acc + u, w[1] * x1 + w[0] * x2)
    u = delay(w[2] * acc + u, w[2] * x1 + w[1] * x2 + w[0] * x3)
    o_ref[...] = _silu(w[3] * acc + u + cb_ref[...]).astype(o_ref.dtype)
    tail_ref[j] = acc[tm - SUBLANE:, :]


def _proj(h2d, w, tm, tn, conv=None):
    t, d = h2d.shape
    n = w.shape[1]
    specs = [pl.BlockSpec((tm, d), lambda i, j: (i, 0)), pl.BlockSpec((d, tn), lambda i, j: (0, j))]
    args, scratch, kern, name = [h2d, w], [], _proj_kernel, "proj"
    if conv is not None:
        conv_w, conv_b, seq = conv
        assert seq % tm == 0
        specs += [pl.BlockSpec((SSM_CONV, tn), lambda i, j: (0, j)), pl.BlockSpec((1, tn), lambda i, j: (0, j))]
        args += [conv_w, conv_b.reshape(1, -1)]
        scratch = [pltpu.VMEM((n // tn, SUBLANE, tn), F32)]
        kern, name = functools.partial(_proj_conv_kernel, tiles_per_seq=seq // tm), "proj_conv"
    return pl.pallas_call(
        kern,
        out_shape=jax.ShapeDtypeStruct((t, n), ACT_DTYPE),
        grid=(t // tm, n // tn),
        in_specs=specs,
        out_specs=pl.BlockSpec((tm, tn), lambda i, j: (i, j)),
        scratch_shapes=scratch,
        compiler_params=pltpu.CompilerParams(dimension_semantics=("arbitrary", "arbitrary"),
                                             vmem_limit_bytes=VMEM_LIMIT),
        name=name,
    )(*args)


def _rope(y, cos_t, sin_t, first_half, half):
    fwd = pltpu.roll(y, LANE - half, 1)
    bwd = pltpu.roll(y, half, 1)
    return y * cos_t + jnp.where(first_half, fwd, bwd) * sin_t


def _prep_kernel(q_ref, k_ref, qi_ref, ki_ref, pos_ref, inv_ref, exp_ref, pass_ref,
                 qg_ref, kg_ref, lg_ref, lb_ref, qo_ref, ko_ref, qio_ref, kio_ref):
    tm = q_ref.shape[0]
    lane = lax.broadcasted_iota(jnp.int32, (tm, LANE), 1)
    ang = pos_ref[...].astype(F32) * inv_ref[...]
    cos_v, sin_v = jnp.cos(ang), jnp.sin(ang)
    ca = _dot_x_sel(cos_v, exp_ref[0]) + pass_ref[0:1, :]
    sa = _dot_x_sel(sin_v, exp_ref[1])
    ci = _dot_x_sel(cos_v, exp_ref[2]) + pass_ref[1:2, :]
    si = _dot_x_sel(sin_v, exp_ref[3])
    att_first = lane < ATT_ROT_DIM // 2
    idx_first = (lane % IDX_DIM) < IDX_ROT_DIM // 2

    def norm_rope(x, g):
        ms = jnp.mean(x * x, axis=-1, keepdims=True)
        y = x * lax.rsqrt(ms + EPS) * g
        return _rope(y, ca, sa, att_first, ATT_ROT_DIM // 2)

    q_scale = ATT_HEAD_DIM ** -0.5 * LOG2_E
    for h in range(ATT_HEADS):
        sl = slice(h * LANE, (h + 1) * LANE)
        qo_ref[:, sl] = (norm_rope(q_ref[:, sl].astype(F32), qg_ref[...]) * q_scale).astype(qo_ref.dtype)
    for h in range(ATT_KV_HEADS):
        sl = slice(h * LANE, (h + 1) * LANE)
        ko_ref[:, sl] = norm_rope(k_ref[:, sl].astype(F32), kg_ref[...]).astype(ko_ref.dtype)
    for j in range(IDX_Q_DIM // LANE):
        sl = slice(j * LANE, (j + 1) * LANE)
        y = _rope(qi_ref[:, sl].astype(F32), ci, si, idx_first, IDX_ROT_DIM // 2)
        qio_ref[:, sl] = (y * IDX_DIM ** -0.5).astype(qio_ref.dtype)
    x = ki_ref[...].astype(F32)
    real = lane < IDX_DIM
    mu = jnp.sum(jnp.where(real, x, 0.0), axis=-1, keepdims=True) * (1.0 / IDX_DIM)
    dlt = jnp.where(real, x - mu, 0.0)
    var = jnp.sum(dlt * dlt, axis=-1, keepdims=True) * (1.0 / IDX_DIM)
    y = dlt * lax.rsqrt(var + EPS) * lg_ref[...] + lb_ref[...]
    kio_ref[...] = _rope(y, ci, si, idx_first, IDX_ROT_DIM // 2).astype(kio_ref.dtype)


def _rope_constants():
    def inv_freq(rot):
        return jnp.power(ROPE_THETA, -(jnp.arange(0, rot, 2, dtype=F32) / rot))

    half_a, half_i = ATT_ROT_DIM // 2, IDX_ROT_DIM // 2
    inv = jnp.zeros((LANE,), F32).at[:half_a].set(inv_freq(ATT_ROT_DIM)).at[half_a:half_a + half_i].set(
        inv_freq(IDX_ROT_DIM))
    src = jnp.arange(LANE)[:, None]
    dst = jnp.arange(LANE)[None, :]
    a_lo = (dst < half_a) & (src == dst)
    a_hi = (dst >= half_a) & (dst < 2 * half_a) & (src == dst - half_a)
    d64 = dst % IDX_DIM
    i_lo = (d64 < half_i) & (src == half_a + d64)
    i_hi = (d64 >= half_i) & (d64 < 2 * half_i) & (src == half_a + d64 - half_i)
    as_f = lambda m: m.astype(F32)
    spread = jnp.stack([as_f(a_lo) + as_f(a_hi), as_f(a_hi) - as_f(a_lo),
                        as_f(i_lo) + as_f(i_hi), as_f(i_hi) - as_f(i_lo)]).astype(jnp.bfloat16)
    lanes = jnp.arange(LANE)
    passthrough = jnp.stack([as_f(lanes >= 2 * half_a), as_f(lanes % IDX_DIM >= 2 * half_i)])
    return inv.reshape(1, LANE), spread, passthrough


def _prep(proj, positions, q_norm_g, k_norm_g, ln_g, ln_b, tm):
    t = proj.shape[0]
    inv, spread, passthrough = _rope_constants()
    row = lambda w, c0: pl.BlockSpec((tm, w), lambda i: (i, c0 // w))
    par = pl.BlockSpec((1, LANE), lambda i: (0, 0))
    pad = lambda v: jnp.pad(v, (0, LANE - v.shape[0])).reshape(1, LANE)
    out = lambda w: pl.BlockSpec((tm, w), lambda i: (i, 0))
    return pl.pallas_call(
        _prep_kernel,
        out_shape=(jax.ShapeDtypeStruct((t, ATT_Q_DIM), MXU_DTYPE),
                   jax.ShapeDtypeStruct((t, ATT_KV_DIM), MXU_DTYPE),
                   jax.ShapeDtypeStruct((t, IDX_Q_DIM), MXU_DTYPE),
                   jax.ShapeDtypeStruct((t, LANE), MXU_DTYPE)),
        grid=(t // tm,),
        in_specs=[row(ATT_Q_DIM, COL_Q), row(ATT_KV_DIM, COL_K), row(IDX_Q_DIM, COL_QIDX), row(LANE, COL_KIDX),
                  pl.BlockSpec((tm, 1), lambda i: (i, 0)), par,
                  pl.BlockSpec(spread.shape, lambda i: (0, 0, 0)), pl.BlockSpec(passthrough.shape, lambda i: (0, 0)),
                  par, par, par, par],
        out_specs=(out(ATT_Q_DIM), out(ATT_KV_DIM), out(IDX_Q_DIM), out(LANE)),
        compiler_params=pltpu.CompilerParams(dimension_semantics=("parallel",),
                                             vmem_limit_bytes=VMEM_LIMIT),
        name="prep",
    )(proj, proj, proj, proj, positions.reshape(t, 1), inv, spread, passthrough,
      q_norm_g.reshape(1, LANE), k_norm_g.reshape(1, LANE), pad(ln_g), pad(ln_b))


def _attn_kernel(q_ref, qiT_ref, smT_ref, kT_ref, v_ref, ki_ref, qg_ref, kg_ref, o_ref,
                 keys_ref, hi_ref, lo_ref, m_ref, acc_ref, qcat_ref, *, tq, topk):
    i = pl.program_id(1)
    nchunk = i + 1
    tk = tq
    w_idx = smT_ref[SMALL_WIDX:SMALL_WIDX + IDX_HEADS, :] * (IDX_HEADS ** -0.5)
    qi_all = jnp.concatenate([qiT_ref[h * IDX_DIM:(h + 1) * IDX_DIM, :] for h in range(IDX_HEADS)], axis=1)
    row_iota = lax.broadcasted_iota(jnp.int32, (tk, tq), 0)
    col_iota = lax.broadcasted_iota(jnp.int32, (tk, tq), 1)

    def score_chunk(c, diagonal):
        r0 = pl.multiple_of(c * tk, tk)
        kc = ki_ref[pl.ds(r0, tk), :][:, :IDX_DIM]
        lg = _dot(kc, qi_all)
        s = w_idx[0:1, :] * jnp.maximum(lg[:, :tq], 0.0)
        for h in range(1, IDX_HEADS):
            s = s + w_idx[h:h + 1, :] * jnp.maximum(lg[:, h * tq:(h + 1) * tq], 0.0)
        bits = lax.bitcast_convert_type(s + 0.0, jnp.int32)
        key = jnp.where(bits < 0, bits ^ jnp.int32(0x7FFFFFFF), bits)
        if diagonal:
            key = jnp.where(row_iota <= col_iota, key, jnp.int32(INT_MIN))
        keys_ref[pl.ds(r0, tk), :] = key
        hi_ref[pl.ds(r0, tk), :] = (key >> 16).astype(jnp.int16)
        lo_ref[pl.ds(r0, tk), :] = ((key & 0xFFFF) - HALF_BIAS).astype(jnp.int16)

    def score_body(c, carry):
        score_chunk(c, False)
        return carry

    lax.fori_loop(0, nchunk - 1, score_body, 0)
    score_chunk(nchunk - 1, True)

    pack = 2 * SUBLANE

    def count16(ref, pred):
        def body(c, acc):
            kk = ref[pl.ds(pl.multiple_of(c * tk, tk), tk), :]
            hit = jnp.where(pred(kk), jnp.bfloat16(1), jnp.bfloat16(0))
            parts = [hit[r * pack:(r + 1) * pack, :] for r in range(tk // pack)]
            while len(parts) > 1:
                parts = [parts[n] + parts[n + 1] for n in range(0, len(parts), 2)]
            return acc + parts[0].astype(F32)
        acc = lax.fori_loop(0, nchunk, body, jnp.zeros((pack, tq), F32))
        return acc.sum(axis=0, keepdims=True)

    def bisect16(ref, want):
        def bit_body(j, ans):
            trial = ans + lax.shift_left(jnp.int32(1), 15 - j)
            t16 = trial.astype(jnp.int16)
            cnt = count16(ref, lambda kk: kk >= t16)
            return jnp.where(cnt >= want, trial, ans)
        return lax.fori_loop(0, 16, bit_body, jnp.full((1, tq), -HALF_BIAS, jnp.int32))

    ans_hi = bisect16(hi_ref, float(topk))
    hi16 = ans_hi.astype(jnp.int16)
    want_lo = float(topk) - count16(hi_ref, lambda kk: kk > hi16)

    def restrict_body(c, carry):
        sl = pl.ds(pl.multiple_of(c * tk, tk), tk)
        lo_ref[sl, :] = jnp.where(hi_ref[sl, :] == hi16, lo_ref[sl, :], jnp.int16(-HALF_BIAS))
        return carry

    lax.fori_loop(0, nchunk, restrict_body, 0)
    ans_lo = bisect16(lo_ref, want_lo)
    ans = lax.shift_left(ans_hi, 16) | (ans_lo + HALF_BIAS)
    thr = jnp.maximum(ans, jnp.int32(INT_MIN + 1))
    lo16 = ans_lo.astype(jnp.int16)
    need_f = want_lo - count16(lo_ref, lambda kk: kk > lo16)

    for g in range(ATT_KV_HEADS):
        qcat_ref[g] = jnp.concatenate(
            [q_ref[:, h * LANE:(h + 1) * LANE] for h in range(g * ATT_GROUP, (g + 1) * ATT_GROUP)], axis=0)
    acc_ref[...] = jnp.zeros(acc_ref.shape, F32)
    ones = jnp.ones((tk, LANE), MXU_DTYPE)

    bound = (ATT_HEAD_DIM ** 0.5 * LOG2_E) * jnp.max(jnp.abs(qg_ref[...])) * jnp.max(jnp.abs(kg_ref[...]))
    bounded = bound <= SAFE_LOG2_LOGIT

    rb = min(tk, MXU_DIM)
    before = jnp.where(lax.broadcasted_iota(jnp.int32, (rb, rb), 1) < lax.broadcasted_iota(jnp.int32, (rb, rb), 0),
                       1.0, 0.0).astype(jnp.bfloat16)

    def tie_bias(c, taken):
        kk = keys_ref[pl.ds(pl.multiple_of(c * tk, tk), tk), :]
        eq = kk == thr
        eq_b = jnp.where(eq, 1.0, 0.0).astype(jnp.bfloat16)
        ranks = []
        for r in range(tk // rb):
            blk = eq_b[r * rb:(r + 1) * rb, :]
            ranks.append(_dot(before, blk) + taken)
            taken = taken + jnp.sum(blk.astype(F32), axis=0, keepdims=True)
        rank = jnp.concatenate(ranks, axis=0)
        tie = jnp.where(eq, jnp.where(rank < need_f, 0.0, NEG_BIG), NEG_BIG)
        bias = jnp.where(kk > thr, 0.0, tie)
        return bias.T, taken

    def masked_logits(c, g, bias_t):
        r0 = pl.multiple_of(c * tk, tk)
        s = _dot(qcat_ref[g], kT_ref[g * LANE:(g + 1) * LANE, pl.ds(r0, tk)])
        vx = jnp.concatenate([v_ref[pl.ds(r0, tk), g * LANE:(g + 1) * LANE], ones], axis=1)
        return [s[r * tq:(r + 1) * tq, :] + bias_t for r in range(ATT_GROUP)], vx

    def plain_softmax(c, bias_t):
        for g in range(ATT_KV_HEADS):
            s, vx = masked_logits(c, g, bias_t)
            p = jnp.concatenate([jnp.exp2(sr).astype(MXU_DTYPE) for sr in s], axis=0)
            acc_ref[g] += _dot(p, vx)

    def online_softmax(c, bias_t):
        for g in range(ATT_KV_HEADS):
            s, vx = masked_logits(c, g, bias_t)
            s = jnp.concatenate(s, axis=0)
            m_old = m_ref[g]
            m_new = jnp.maximum(m_old, jnp.max(s, axis=1, keepdims=True))
            p = jnp.exp2(s - m_new)
            acc_ref[g] = jnp.exp2(m_old - m_new) * acc_ref[g] + _dot(p.astype(MXU_DTYPE), vx)
            m_ref[g] = m_new

    def run(bias_fn, softmax_fn):
        def body(c, taken):
            bias_t, taken = bias_fn(c, taken)
            softmax_fn(c, bias_t)
            return taken
        lax.fori_loop(0, nchunk, body, jnp.zeros((1, tq), F32))

    @pl.when(bounded)
    def _():
        run(tie_bias, plain_softmax)

    @pl.when(jnp.logical_not(bounded))
    def _():
        m_ref[...] = jnp.full(m_ref.shape, NEG_BIG, F32)
        run(tie_bias, online_softmax)

    for h in range(ATT_HEADS):
        g, rs = h // ATT_GROUP, slice((h % ATT_GROUP) * tq, (h % ATT_GROUP + 1) * tq)
        o_ref[:, h * LANE:(h + 1) * LANE] = (acc_ref[g, rs, :LANE] / acc_ref[g, rs, LANE:]).astype(o_ref.dtype)


def _attn(q, qiT, smallT, kT, proj3, ki, q_norm_g, k_norm_g, tq):
    bsz, s, _ = q.shape
    topk = min(TOPK_MAX, s // 4)
    kern = functools.partial(_attn_kernel, tq=tq, topk=topk)
    qtile_t = lambda r: pl.BlockSpec((None, r, tq), lambda b, i: (b, 0, i))
    par = pl.BlockSpec((1, LANE), lambda b, i: (0, 0))
    return pl.pallas_call(
        kern,
        out_shape=jax.ShapeDtypeStruct((bsz, s, ATT_Q_DIM), ACT_DTYPE),
        grid=(bsz, s // tq),
        in_specs=[pl.BlockSpec((None, tq, ATT_Q_DIM), lambda b, i: (b, i, 0)),
                  qtile_t(IDX_Q_DIM), qtile_t(LANE),
                  pl.BlockSpec((None, ATT_KV_DIM, s), lambda b, i: (b, 0, 0)),
                  pl.BlockSpec((None, s, ATT_KV_DIM), lambda b, i: (b, 0, COL_V // ATT_KV_DIM)),
                  pl.BlockSpec((None, s, LANE), lambda b, i: (b, 0, 0)), par, par],
        out_specs=pl.BlockSpec((None, tq, ATT_Q_DIM), lambda b, i: (b, i, 0)),
        scratch_shapes=[pltpu.VMEM((s, tq), jnp.int32),
                        pltpu.VMEM((s, tq), jnp.int16),
                        pltpu.VMEM((s, tq), jnp.int16),
                        pltpu.VMEM((ATT_KV_HEADS, ATT_GROUP * tq, 1), F32),
                        pltpu.VMEM((ATT_KV_HEADS, ATT_GROUP * tq, 2 * LANE), F32),
                        pltpu.VMEM((ATT_KV_HEADS, ATT_GROUP * tq, ATT_HEAD_DIM), MXU_DTYPE)],
        compiler_params=pltpu.CompilerParams(dimension_semantics=("parallel", "arbitrary"),
                                             vmem_limit_bytes=VMEM_LIMIT),
        name="attn",
    )(q, qiT, smallT, kT, proj3, ki, q_norm_g.reshape(1, LANE), k_norm_g.reshape(1, LANE))


def _ssm_kernel(xbc_ref, sm_ref, smT_ref, dtb_ref, dtbT_ref, a_ref, aT_ref,
                dsk_ref, exp_ref, y_ref, st_ref, yacc_ref):
    ln = SSM_CHUNK
    c = pl.program_id(1)

    @pl.when(c == 0)
    def _():
        st_ref[...] = jnp.zeros(st_ref.shape, F32)

    xs = xbc_ref[:, :SSM_D_INNER].astype(F32)
    b_all = xbc_ref[:, SSM_D_INNER:SSM_D_INNER + SSM_GROUPS * SSM_STATE].astype(MXU_DTYPE)
    c_all = xbc_ref[:, SSM_D_INNER + SSM_GROUPS * SSM_STATE:].astype(MXU_DTYPE)

    dt = _softplus(sm_ref[...] + dtb_ref[...])
    adt = dt * a_ref[...]
    r_i = lax.broadcasted_iota(jnp.int32, (ln, ln), 0)
    c_i = lax.broadcasted_iota(jnp.int32, (ln, ln), 1)
    tril = r_i >= c_i
    incl = jnp.where(tril, 1.0, 0.0).astype(jnp.bfloat16)
    a_col = _dot_sel_x(incl, adt)
    dtT = _softplus(smT_ref[...] + dtbT_ref[...])
    incl_t = jnp.where(r_i <= c_i, 1.0, 0.0).astype(jnp.bfloat16)
    a_row = _dot_x_sel(dtT * aT_ref[...], incl_t)

    expand = exp_ref[...]
    wide = _dot_x_sel(jnp.concatenate([a_col, dt], axis=0), expand, pieces=2)
    a_wide, dt_wide = wide[:ln], wide[ln:]
    a_last = a_wide[ln - 1:ln, :]
    xc = xs * dt_wide
    xc_b = xc.astype(MXU_DTYPE)
    xd_b = (xc * jnp.exp(a_last - a_wide)).astype(MXU_DTYPE)
    e_wide = jnp.exp(a_wide)
    chunk_decay = e_wide[ln - 1:ln, :]

    lane = lax.broadcasted_iota(jnp.int32, (ln, LANE), 1)
    low = lane < SSM_HEAD_DIM
    gw = SSM_D_INNER // SSM_GROUPS
    for g in range(SSM_GROUPS):
        gs = slice(g * gw, (g + 1) * gw)
        b_g = b_all[:, g * SSM_STATE:(g + 1) * SSM_STATE]
        c_g = c_all[:, g * SSM_STATE:(g + 1) * SSM_STATE]
        cb = lax.dot_general(c_g, b_g, (((1,), (1,)), ((), ())), preferred_element_type=F32)
        y_off = _dot(c_g, st_ref[:, gs].astype(MXU_DTYPE)) * e_wide[:, gs]
        st_ref[:, gs] = chunk_decay[:, gs] * st_ref[:, gs] + lax.dot_general(
            b_g, xd_b[:, gs], (((0,), (0,)), ((), ())), preferred_element_type=F32)
        yacc_ref[:, gs] = y_off
        for jp in range(gw // LANE):
            j = g * (gw // LANE) + jp
            xp = xc_b[:, j * LANE:(j + 1) * LANE]
            yd = None
            for half, keep in ((0, low), (1, jnp.logical_not(low))):
                hd = 2 * j + half
                diff = a_col[:, hd:hd + 1] - a_row[hd:hd + 1, :]
                lmat = jnp.exp(jnp.where(tril, diff, -jnp.inf))
                mh = (cb * lmat).astype(MXU_DTYPE)
                part = _dot(mh, jnp.where(keep, xp, jnp.zeros_like(xp)))
                yd = part if yd is None else yd + part
            yacc_ref[:, j * LANE:(j + 1) * LANE] += yd

    y_ref[...] = (yacc_ref[...] + dsk_ref[...] * xs).astype(y_ref.dtype)


def _ssm(xbc3, small, smallT, dt_bias, a_log, d_skip):
    bsz, s, _ = xbc3.shape
    ln = SSM_CHUNK
    padl = lambda v: jnp.pad(v, (0, LANE - v.shape[0]))
    a = padl(-jnp.exp(a_log))
    dtb = padl(dt_bias)
    head_of = jnp.arange(SSM_D_INNER) // SSM_HEAD_DIM
    expand = (jnp.arange(LANE)[:, None] == head_of[None, :]).astype(jnp.bfloat16)
    full = lambda shp: pl.BlockSpec(shp, lambda b, c: (0,) * len(shp))
    return pl.pallas_call(
        _ssm_kernel,
        out_shape=jax.ShapeDtypeStruct((bsz, s, SSM_D_INNER), ACT_DTYPE),
        grid=(bsz, s // ln),
        in_specs=[pl.BlockSpec((None, ln, SSM_CONV_DIM), lambda b, c: (b, c, 0)),
                  pl.BlockSpec((None, ln, LANE), lambda b, c: (b, c, 0)),
                  pl.BlockSpec((None, LANE, ln), lambda b, c: (b, 0, c)),
                  full((1, LANE)), full((LANE, 1)), full((1, LANE)), full((LANE, 1)),
                  full((1, SSM_D_INNER)), full((LANE, SSM_D_INNER))],
        out_specs=pl.BlockSpec((None, ln, SSM_D_INNER), lambda b, c: (b, c, 0)),
        scratch_shapes=[pltpu.VMEM((SSM_STATE, SSM_D_INNER), F32),
                        pltpu.VMEM((ln, SSM_D_INNER), F32)],
        compiler_params=pltpu.CompilerParams(dimension_semantics=("parallel", "arbitrary"),
                                             vmem_limit_bytes=VMEM_LIMIT),
        name="ssm",
    )(xbc3, small, smallT,
      dtb.reshape(1, LANE), dtb.reshape(LANE, 1), a.reshape(1, LANE), a.reshape(LANE, 1),
      jnp.repeat(d_skip, SSM_HEAD_DIM).reshape(1, -1), expand)


def _final_kernel(o_ref, za_ref, y_ref, zs_ref, ng_ref, ga_ref, gs_ref, x_ref, gate_ref, wa_ref, ws_ref, wo_ref,
                  out_ref):
    o = (o_ref[...].astype(F32) * _silu(za_ref[...].astype(F32))).astype(MXU_DTYPE)
    y_att = _dot(o, wa_ref[...])
    y = y_ref[...].astype(F32) * _silu(zs_ref[...].astype(F32))
    gw = SSM_D_INNER // SSM_GROUPS
    y_n = []
    for g in range(SSM_GROUPS):
        yg = y[:, g * gw:(g + 1) * gw]
        ms = jnp.mean(yg * yg, axis=-1, keepdims=True)
        y_n.append((yg * lax.rsqrt(ms + EPS) * ng_ref[:, g * gw:(g + 1) * gw]).astype(MXU_DTYPE))
    y_ssm = _dot(jnp.concatenate(y_n, axis=1), ws_ref[...])
    merged = _sigmoid(ga_ref[...].astype(F32)) * y_att + _sigmoid(gs_ref[...].astype(F32)) * y_ssm
    out_ref[...] = x_ref[...] + gate_ref[...] * _dot(merged.astype(MXU_DTYPE), wo_ref[...])


def _final(o3, proj3, y3, x, gate, ssm_norm_g, w_att, w_ssm, w_out, tm):
    bsz, s, d = x.shape
    tile = lambda w, c0: pl.BlockSpec((None, tm, w), lambda b, i: (b, i, c0 // w))
    full = lambda shp: pl.BlockSpec(shp, lambda b, i: (0,) * len(shp))
    return pl.pallas_call(
        _final_kernel,
        out_shape=jax.ShapeDtypeStruct((bsz, s, d), x.dtype),
        grid=(bsz, s // tm),
        in_specs=[tile(ATT_Q_DIM, 0), tile(ATT_Q_DIM, COL_ZATT), tile(SSM_D_INNER, 0),
                  tile(SSM_D_INNER, COL_ZSSM), full((1, SSM_D_INNER)),
                  tile(D_MODEL, COL_GATE), tile(D_MODEL, COL_GATE + D_MODEL), tile(d, 0),
                  pl.BlockSpec((None, 1, d), lambda b, i: (b, 0, 0)),
                  full((ATT_Q_DIM, d)), full((SSM_D_INNER, d)), full((d, d))],
        out_specs=tile(d, 0),
        compiler_params=pltpu.CompilerParams(dimension_semantics=("parallel", "parallel"),
                                             vmem_limit_bytes=VMEM_LIMIT),
        name="final",
    )(o3, proj3, y3, proj3, ssm_norm_g.reshape(1, -1), proj3, proj3, x, gate[:, None, :],
      w_att.astype(MXU_DTYPE), w_ssm.astype(MXU_DTYPE), w_out.astype(MXU_DTYPE))


def _permute_w_in(w_in):
    sizes = (ATT_Q_DIM, ATT_KV_DIM, ATT_KV_DIM, ATT_Q_DIM, IDX_Q_DIM, IDX_DIM, IDX_HEADS,
             SSM_D_INNER, SSM_CONV_DIM, SSM_HEADS, 2 * D_MODEL)
    offs = [0]
    for sz in sizes:
        offs.append(offs[-1] + sz)
    cols = lambda w, n: w[:, offs[n]:offs[n + 1]]
    d = w_in.shape[0]
    w_cast = w_in.astype(MXU_DTYPE)
    q, k, v, z_att, q_idx, k_idx, _, z_ssm, xbc, _, gate = (cols(w_cast, n) for n in range(len(sizes)))
    wide = jnp.concatenate(
        [z_ssm, gate, z_att, q, k, v, q_idx, k_idx,
         jnp.zeros((d, WIDE_COLS - COL_KIDX - IDX_DIM), MXU_DTYPE)], axis=1)
    small = jnp.concatenate([cols(w_in, 9), cols(w_in, 6), jnp.zeros((d, LANE - SSM_HEADS - IDX_HEADS), w_in.dtype)],
                            axis=1)
    return xbc, wide, small


def _tile(n, pref):
    return pref if n % pref == 0 else n


def kernel(x, c, positions, ada_w, ada_b, norm_g, w_in, q_norm_g, k_norm_g, idx_k_ln_g, idx_k_ln_b,
           conv_w, conv_b, dt_bias, a_log, d_skip, ssm_norm_g, w_branch_att, w_branch_ssm, w_out):
    bsz, s, d = x.shape
    t = bsz * s
    for l in range(ada_w.shape[0]):
        mod = _ada(c, ada_w[l], ada_b[l])
        shift, scale, gate = mod[:, :d], mod[:, d:2 * d], mod[:, 2 * d:]
        w_xbc, w_wide, w_small = _permute_w_in(w_in[l])
        h, small = _norm(x, norm_g[l], scale, shift, w_small, _tile(s, 512))
        tm = _tile(s, 1024)
        xbc3 = _proj(h.reshape(t, d), w_xbc, tm, 1024, conv=(conv_w[l], conv_b[l], s)).reshape(bsz, s, -1)
        proj = _proj(h.reshape(t, d), w_wide, tm, WIDE_COLS // 4)
        proj3 = proj.reshape(bsz, s, WIDE_COLS)

        q, k, q_idx, k_idx = _prep(proj, positions, q_norm_g[l], k_norm_g[l], idx_k_ln_g[l], idx_k_ln_b[l],
                                   _tile(t, 512))
        to_t = lambda a: jnp.swapaxes(a.reshape(bsz, s, -1), 1, 2)
        smallT = jnp.swapaxes(small, 1, 2)
        o3 = _attn(q.reshape(bsz, s, -1), to_t(q_idx), smallT, to_t(k), proj3, k_idx.reshape(bsz, s, LANE),
                   q_norm_g[l], k_norm_g[l], _tile(s, 512))

        y3 = _ssm(xbc3, small, smallT, dt_bias[l], a_log[l], d_skip[l])
        x = _final(o3, proj3, y3, x, gate, ssm_norm_g[l], w_branch_att[l], w_branch_ssm[l], w_out[l], _tile(s, 512))
    return x
```
